```python
import jax, jax.numpy as jnp
from jax import lax
import numpy as np


D_MODEL = 2048
BATCH = 4
SEQ = 4096
DEPTH = 2
DEC_BATCH = 2
DEC_SEQ = 4096
PAST_LEN = 128

N_EVEN = (DEPTH + 1) // 2
N_ODD = DEPTH // 2
HEAD_DIM = 128
D_MIX = D_MODEL
D_A = D_MIX // 2
D_B = D_MIX - D_A
N_GROUPS_B = D_B // HEAD_DIM
SHORT_CONV_W = 3
D_C = D_MIX // 2
D_D = D_MIX - D_C
CONF_CONV_W = 31
POOL_WINDOWS = (2, 4, 8, 16)
N_POOL = len(POOL_WINDOWS)
D_POOL_G = D_D // N_POOL
PEER_HEADS = 8
PEER_NKEYS = 128
PEER_N_EXPERTS = PEER_NKEYS * PEER_NKEYS
PEER_DQ = 256
PEER_DHALF = PEER_DQ // 2
PEER_TOPK = 16
PEER_BLOCK = 128
NORM_EPS = 1e-6

kernel_name = 'hybrid_conv_fourier_pool_peer_encoder'


def rmsnorm(x, g):
    xf = x.astype(jnp.float32)
    r = lax.rsqrt(jnp.mean(xf * xf, axis=-1, keepdims=True) + NORM_EPS)
    return (xf * r).astype(x.dtype) * g


def layernorm(x, g, b):
    xf = x.astype(jnp.float32)
    mu = jnp.mean(xf, axis=-1, keepdims=True)
    var = jnp.mean(jnp.square(xf - mu), axis=-1, keepdims=True)
    return ((xf - mu) * lax.rsqrt(var + NORM_EPS)).astype(x.dtype) * g + b


def depthwise_conv(x, w, b):
    k = w.shape[0]
    y = lax.conv_general_dilated(
        x, w[:, None, :].astype(x.dtype), window_strides=(1,),
        padding=[(k // 2, k // 2)], dimension_numbers=('NWC', 'WIO', 'NWC'),
        feature_group_count=x.shape[-1])
    return y + b


def centred_window_mean(z):
    s = z.shape[1]
    cs = jnp.cumsum(z.astype(jnp.float32), axis=1)
    cs = jnp.pad(cs, ((0, 0), (1, 0), (0, 0), (0, 0)))
    t = jnp.arange(s)
    outs = []
    for gi, w in enumerate(POOL_WINDOWS):
        left = w // 2
        right = w - 1 - left
        lo = jnp.clip(t - left, 0, s)
        hi = jnp.clip(t + right + 1, 0, s)
        csg = cs[:, :, gi]
        tot = jnp.take(csg, hi, axis=1) - jnp.take(csg, lo, axis=1)
        outs.append(tot / (hi - lo).astype(jnp.float32)[None, :, None])
    return jnp.stack(outs, axis=2).astype(z.dtype)


def mixer_ab(h, in_w, conv_w, conv_b, out_w):
    p = h @ in_w
    gb, gc, v, f = jnp.split(p, [D_A, 2 * D_A, 3 * D_A], axis=-1)
    ya = gb * depthwise_conv(gc * v, conv_w, conv_b)
    bsz, s, _ = f.shape
    fg = f.reshape(bsz, s, N_GROUPS_B, HEAD_DIM).astype(jnp.float32)
    yb = jnp.real(jnp.fft.fft2(fg, axes=(1, 3), norm='ortho'))
    yb = yb.reshape(bsz, s, D_B).astype(h.dtype)
    return jnp.concatenate([ya, yb], axis=-1) @ out_w


def mixer_cd(h, in_w, in_b, conv_w, conv_b, ln_g, ln_b, pool_w, pool_scale, out_w):
    p = h @ in_w + in_b
    a, g, z = jnp.split(p, [D_C, 2 * D_C], axis=-1)
    u = a * jax.nn.sigmoid(g)
    u = depthwise_conv(u, conv_w, conv_b)
    u = jax.nn.silu(layernorm(u, ln_g, ln_b))
    bsz, s, _ = z.shape
    zg = z.reshape(bsz, s, N_POOL, D_POOL_G)
    pooled = centred_window_mean(zg) - zg
    yd = jnp.einsum('bsgc,gcd->bsgd', pooled, pool_w).reshape(bsz, s, D_D) * pool_scale
    return jnp.concatenate([u, yd], axis=-1) @ out_w


def peer(h, wq, sub_keys, u_tab, v_tab):
    bsz, s, d = h.shape
    n_tok = bsz * s
    x = h.reshape(n_tok, d)
    q = (x @ wq).reshape(n_tok, PEER_HEADS, 2, PEER_DHALF)
    sc = jnp.einsum('thpk,hpnk->thpn', q, sub_keys).astype(jnp.float32)
    s1, i1 = lax.top_k(sc[:, :, 0], PEER_TOPK)
    s2, i2 = lax.top_k(sc[:, :, 1], PEER_TOPK)
    cand_s = (s1[..., :, None] + s2[..., None, :]).reshape(n_tok, PEER_HEADS, PEER_TOPK * PEER_TOPK)
    cand_i = (i1[..., :, None] * PEER_NKEYS + i2[..., None, :]).reshape(n_tok, PEER_HEADS, PEER_TOPK * PEER_TOPK)
    top_s, pos = lax.top_k(cand_s, PEER_TOPK)
    idx = jnp.take_along_axis(cand_i, pos, axis=-1)
    gate = jax.nn.softmax(top_s, axis=-1).astype(h.dtype)
    n_blk = n_tok // PEER_BLOCK
    xb = x.reshape(n_blk, PEER_BLOCK, d)
    ib = idx.reshape(n_blk, PEER_BLOCK, PEER_HEADS * PEER_TOPK)
    gb = gate.reshape(n_blk, PEER_BLOCK, PEER_HEADS * PEER_TOPK)

    def one_block(args):
        xs, es, gs = args
        u = jnp.take(u_tab, es, axis=0)
        act = jax.nn.gelu(jnp.einsum('td,ted->te', xs, u), approximate=False) * gs
        v = jnp.take(v_tab, es, axis=0)
        return jnp.einsum('te,ted->td', act, v)

    y = lax.map(one_block, (xb, ib, gb))
    return y.reshape(bsz, s, d)


def encode(x, c, ada_w, ada_b, norm_mix, norm_ffn,
           ab_in_w, ab_conv_w, ab_conv_b, ab_out_w,
           cd_in_w, cd_in_b, cd_conv_w, cd_conv_b, cd_ln_g, cd_ln_b, cd_pool_w, cd_pool_scale, cd_out_w,
           peer_wq, peer_keys, peer_u, peer_v, final_norm):
    for i in range(DEPTH):
        mod = jax.nn.silu(c) @ ada_w[i] + ada_b[i]
        sh_m, sc_m, g_m, sh_f, sc_f, g_f = [m[:, None, :] for m in jnp.split(mod, 6, axis=-1)]
        h = rmsnorm(x, norm_mix[i]) * (1 + sc_m) + sh_m
        j = i // 2
        if i % 2 == 0:
            mix = mixer_ab(h, ab_in_w[j], ab_conv_w[j], ab_conv_b[j], ab_out_w[j])
        else:
            mix = mixer_cd(h, cd_in_w[j], cd_in_b[j], cd_conv_w[j], cd_conv_b[j], cd_ln_g[j], cd_ln_b[j],
                           cd_pool_w[j], cd_pool_scale[j], cd_out_w[j])
        x = x + g_m * mix
        h = rmsnorm(x, norm_ffn[i]) * (1 + sc_f) + sh_f
        x = x + g_f * peer(h, peer_wq[i], peer_keys[i], peer_u[i], peer_v[i])
    return rmsnorm(x, final_norm)


def setup_inputs(seed: int = 0) -> dict:
    key = jax.random.key(seed)
    ks = jax.random.split(key, 32)

    def nrm(k, shape, std):
        return jax.random.normal(k, shape, jnp.float32) * std

    d = D_MODEL
    return {
        'x_prompt': nrm(ks[0], (BATCH, SEQ, d), 1.0),
        'x_sample': nrm(ks[1], (DEC_BATCH, DEC_SEQ, d), 1.0),
        'c_prompt': nrm(ks[2], (BATCH, d), 1.0),
        'c_sample': nrm(ks[3], (DEC_BATCH, d), 1.0),
        'ada_w': nrm(ks[4], (DEPTH, d, 6 * d), 0.5 * d ** -0.5),
        'ada_b': nrm(ks[5], (DEPTH, 6 * d), 0.02),
        'norm_mix': 1.0 + nrm(ks[6], (DEPTH, d), 0.02),
        'norm_ffn': 1.0 + nrm(ks[7], (DEPTH, d), 0.02),
        'ab_in_w': nrm(ks[8], (N_EVEN, d, 3 * D_A + D_B), d ** -0.5),
        'ab_conv_w': nrm(ks[9], (N_EVEN, SHORT_CONV_W, D_A), SHORT_CONV_W ** -0.5),
        'ab_conv_b': nrm(ks[10], (N_EVEN, D_A), 0.02),
        'ab_out_w': nrm(ks[11], (N_EVEN, D_MIX, d), D_MIX ** -0.5),
        'cd_in_w': nrm(ks[12], (N_ODD, d, 2 * D_C + D_D), d ** -0.5),
        'cd_in_b': nrm(ks[13], (N_ODD, 2 * D_C + D_D), 0.02),
        'cd_conv_w': nrm(ks[14], (N_ODD, CONF_CONV_W, D_C), CONF_CONV_W ** -0.5),
        'cd_conv_b': nrm(ks[15], (N_ODD, D_C), 0.02),
        'cd_ln_g': 1.0 + nrm(ks[16], (N_ODD, D_C), 0.02),
        'cd_ln_b': nrm(ks[17], (N_ODD, D_C), 0.02),
        'cd_pool_w': nrm(ks[18], (N_ODD, N_POOL, D_POOL_G, D_POOL_G), D_POOL_G ** -0.5),
        'cd_pool_scale': 1.0 + nrm(ks[19], (N_ODD, D_D), 0.1),
        'cd_out_w': nrm(ks[20], (N_ODD, D_MIX, d), D_MIX ** -0.5),
        'peer_wq': nrm(ks[21], (DEPTH, d, PEER_HEADS * PEER_DQ), d ** -0.5),
        'peer_keys': nrm(ks[22], (DEPTH, PEER_HEADS, 2, PEER_NKEYS, PEER_DHALF), PEER_DHALF ** -0.5),
        'peer_u': nrm(ks[23], (DEPTH, PEER_N_EXPERTS, d), d ** -0.5),
        'peer_v': nrm(ks[24], (DEPTH, PEER_N_EXPERTS, d), 0.5),
        'final_norm': 1.0 + nrm(ks[25], (d,), 0.02),
    }


def reference(x_prompt, x_sample, c_prompt, c_sample, ada_w, ada_b, norm_mix, norm_ffn,
              ab_in_w, ab_conv_w, ab_conv_b, ab_out_w,
              cd_in_w, cd_in_b, cd_conv_w, cd_conv_b, cd_ln_g, cd_ln_b, cd_pool_w, cd_pool_scale, cd_out_w,
              peer_wq, peer_keys, peer_u, peer_v, final_norm):
    y_prompt = encode(x_prompt, c_prompt, ada_w, ada_b, norm_mix, norm_ffn,
                      ab_in_w, ab_conv_w, ab_conv_b, ab_out_w,
                      cd_in_w, cd_in_b, cd_conv_w, cd_conv_b, cd_ln_g, cd_ln_b, cd_pool_w, cd_pool_scale, cd_out_w,
                      peer_wq, peer_keys, peer_u, peer_v, final_norm)
    y_sample = encode(x_sample, c_sample, ada_w, ada_b, norm_mix, norm_ffn,
                      ab_in_w, ab_conv_w, ab_conv_b, ab_out_w,
                      cd_in_w, cd_in_b, cd_conv_w, cd_conv_b, cd_ln_g, cd_ln_b, cd_pool_w, cd_pool_scale, cd_out_w,
                      peer_wq, peer_keys, peer_u, peer_v, final_norm)
    return (y_prompt, y_sample)
```

```python
import functools
import math

import jax
import jax.numpy as jnp
from jax import lax
from jax.experimental import pallas as pl
from jax.experimental.pallas import tpu as pltpu

F32 = jnp.float32
BF16 = jnp.bfloat16

LANES = 128
BF16_SUBLANES = 16
VMEM_LIMIT_BYTES = 56 * 1024 * 1024

HEAD_DIM = 128
SHORT_CONV_W = 3
CONF_CONV_W = 31
POOL_WINDOWS = (2, 4, 8, 16)
PEER_TOPK = 16
NORM_EPS = 1e-6
SQRT_HALF = 0.7071067811865476

HALO = BF16_SUBLANES
assert CONF_CONV_W // 2 < HALO and max(POOL_WINDOWS) // 2 <= HALO

NT_DIMS = (((1,), (1,)), ((), ()))


def _cparams(*semantics):
    return pltpu.CompilerParams(dimension_semantics=semantics, vmem_limit_bytes=VMEM_LIMIT_BYTES)


def _resident(block_shape, index_map):
    return pl.BlockSpec(block_shape, index_map, pipeline_mode=pl.Buffered(1))


def _sigmoid(x):
    return 1.0 / (1.0 + jnp.exp(-x))


def _norm_mod(x, g, sc, sh):
    r = lax.rsqrt(jnp.mean(x * x, axis=-1, keepdims=True) + NORM_EPS)
    return (x * r) * g * (1.0 + sc) + sh


def _mod_kernel(c_ref, w_ref, b_ref, o_ref):
    c = c_ref[...]
    a = (c * _sigmoid(c)).astype(BF16)
    o_ref[0] = jnp.dot(a, w_ref[0].astype(BF16), preferred_element_type=F32) + b_ref[0]


def _modulation(c_pad, ada_w, ada_b, *, tn=1024):
    depth, d, n = ada_w.shape
    rows = c_pad.shape[0]
    return pl.pallas_call(
        _mod_kernel,
        grid=(depth, n // tn),
        in_specs=[
            pl.BlockSpec((rows, d), lambda l, j: (0, 0)),
            pl.BlockSpec((1, d, tn), lambda l, j: (l, 0, j)),
            pl.BlockSpec((1, 1, tn), lambda l, j: (l, 0, j)),
        ],
        out_specs=pl.BlockSpec((1, rows, tn), lambda l, j: (l, 0, j)),
        out_shape=jax.ShapeDtypeStruct((depth, rows, n), F32),
        compiler_params=_cparams("parallel", "parallel"),
        name="adaln_modulation",
    )(c_pad, ada_w, ada_b.reshape(depth, 1, n))


def _inproj_kernel(x_ref, g_ref, sc_ref, sh_ref, w_ref, b_ref, o_ref, *, n_chunk):
    h = _norm_mod(x_ref[...], g_ref[...], sc_ref[0], sh_ref[0]).astype(BF16)
    for n0 in range(0, o_ref.shape[1], n_chunk):
        acc = jnp.dot(h, w_ref[:, n0:n0 + n_chunk], preferred_element_type=F32)
        o_ref[:, n0:n0 + n_chunk] = (acc + b_ref[:, n0:n0 + n_chunk]).astype(o_ref.dtype)


def _inproj(x, g, sc, sh, w, b, *, seq, tm=512, n_chunk=512):
    t, d = x.shape
    n = w.shape[1]
    tps = seq // tm
    return pl.pallas_call(
        functools.partial(_inproj_kernel, n_chunk=n_chunk),
        grid=(t // tm,),
        in_specs=[
            pl.BlockSpec((tm, d), lambda i: (i, 0)),
            _resident((1, d), lambda i: (0, 0)),
            pl.BlockSpec((1, 1, d), lambda i: (i // tps, 0, 0)),
            pl.BlockSpec((1, 1, d), lambda i: (i // tps, 0, 0)),
            _resident((d, n), lambda i: (0, 0)),
            _resident((1, n), lambda i: (0, 0)),
        ],
        out_specs=pl.BlockSpec((tm, n), lambda i: (i, 0)),
        out_shape=jax.ShapeDtypeStruct((t, n), BF16),
        compiler_params=_cparams("parallel"),
        name="norm_inproj",
    )(x, g, sc, sh, w, b)


def _outproj_kernel(l_ref, r_ref, w_ref, x_ref, gate_ref, o_ref, *, n_chunk):
    half = l_ref.shape[1]
    l = l_ref[...]
    r = r_ref[...]
    for n0 in range(0, o_ref.shape[1], n_chunk):
        acc = jnp.dot(l, w_ref[:half, n0:n0 + n_chunk], preferred_element_type=F32)
        acc += jnp.dot(r, w_ref[half:, n0:n0 + n_chunk], preferred_element_type=F32)
        o_ref[:, n0:n0 + n_chunk] = x_ref[:, n0:n0 + n_chunk] + gate_ref[0][:, n0:n0 + n_chunk] * acc


def _outproj(l, r, w, x, gate, *, seq, tm=512, n_chunk=512):
    t, d = x.shape
    half = l.shape[1]
    tps = seq // tm
    return pl.pallas_call(
        functools.partial(_outproj_kernel, n_chunk=n_chunk),
        grid=(t // tm,),
        in_specs=[
            pl.BlockSpec((tm, half), lambda i: (i, 0)),
            pl.BlockSpec((tm, half), lambda i: (i, 0)),
            _resident((2 * half, d), lambda i: (0, 0)),
            pl.BlockSpec((tm, d), lambda i: (i, 0)),
            pl.BlockSpec((1, 1, d), lambda i: (i // tps, 0, 0)),
        ],
        out_specs=pl.BlockSpec((tm, d), lambda i: (i, 0)),
        out_shape=jax.ShapeDtypeStruct((t, d), F32),
        compiler_params=_cparams("parallel"),
        name="outproj_residual",
    )(l, r, w, x, gate)


def _gconv3_kernel(gb_ref, gc_ref, v_ref, gcp_ref, vp_ref, gcn_ref, vn_ref, w_ref, b_ref, o_ref, *, tps):
    tm = gb_ref.shape[0]
    si = pl.program_id(0) % tps
    u = gc_ref[...].astype(F32) * v_ref[...].astype(F32)
    u_prev = (gcp_ref[...].astype(F32) * vp_ref[...].astype(F32))[HALO - 1:HALO, :]
    u_next = (gcn_ref[...].astype(F32) * vn_ref[...].astype(F32))[0:1, :]
    u_prev = jnp.where(si == 0, 0.0, u_prev)
    u_next = jnp.where(si == tps - 1, 0.0, u_next)
    row = lax.broadcasted_iota(jnp.int32, (tm, 1), 0)
    u_m1 = jnp.where(row == 0, u_prev, pltpu.roll(u, 1, 0))
    u_p1 = jnp.where(row == tm - 1, u_next, pltpu.roll(u, tm - 1, 0))
    y = w_ref[0:1, :] * u_m1 + w_ref[1:2, :] * u + w_ref[2:3, :] * u_p1 + b_ref[...]
    o_ref[...] = (gb_ref[...].astype(F32) * y).astype(o_ref.dtype)


def _gconv3(p, conv_w, conv_b, *, seq, d_a, tm=256):
    t = p.shape[0]
    tps = seq // tm
    hb = tm // HALO
    last_hb = t // HALO - 1
    main = lambda col: pl.BlockSpec((tm, d_a), lambda i: (i, col))
    prev = lambda col: pl.BlockSpec((HALO, d_a), lambda i: (jnp.maximum(i * hb - 1, 0), col))
    nxt = lambda col: pl.BlockSpec((HALO, d_a), lambda i: (jnp.minimum((i + 1) * hb, last_hb), col))
    return pl.pallas_call(
        functools.partial(_gconv3_kernel, tps=tps),
        grid=(t // tm,),
        in_specs=[main(0), main(1), main(2), prev(1), prev(2), nxt(1), nxt(2),
                  _resident((SHORT_CONV_W, d_a), lambda i: (0, 0)),
                  _resident((1, d_a), lambda i: (0, 0))],
        out_specs=pl.BlockSpec((tm, d_a), lambda i: (i, 0)),
        out_shape=jax.ShapeDtypeStruct((t, d_a), BF16),
        compiler_params=_cparams("parallel"),
        name="gated_conv3",
    )(p, p, p, p, p, p, p, conv_w, conv_b)


def _dft_channel_kernel(f_ref, cs_ref, o_ref):
    for g in range(f_ref.shape[1] // HEAD_DIM):
        c0 = g * HEAD_DIM
        r = jnp.dot(f_ref[:, c0:c0 + HEAD_DIM], cs_ref[...], preferred_element_type=F32)
        o_ref[0, 0, :, c0:c0 + HEAD_DIM] = r[:, :HEAD_DIM].astype(o_ref.dtype)
        o_ref[0, 1, :, c0:c0 + HEAD_DIM] = r[:, HEAD_DIM:].astype(o_ref.dtype)


def _dft_channel(p, cs, *, batch, seq, d_b, col_block, tm=512):
    tps = seq // tm
    return pl.pallas_call(
        _dft_channel_kernel,
        grid=(batch * tps,),
        in_specs=[pl.BlockSpec((tm, d_b), lambda i: (i, col_block)),
                  _resident((HEAD_DIM, 2 * HEAD_DIM), lambda i: (0, 0))],
        out_specs=pl.BlockSpec((1, 2, tm, d_b), lambda i: (i // tps, 0, i % tps, 0)),
        out_shape=jax.ShapeDtypeStruct((batch, 2, seq, d_b), BF16),
        compiler_params=_cparams("parallel"),
        name="dft_channels",
    )(p, cs)


def _dft_seq_kernel(d_ref, ab_ref, o_ref, acc_ref):
    k = pl.program_id(2)

    @pl.when(k == 0)
    def _():
        acc_ref[...] = jnp.zeros_like(acc_ref)

    acc_ref[...] += jnp.dot(d_ref[...], ab_ref[0], preferred_element_type=F32)

    @pl.when(k == pl.num_programs(2) - 1)
    def _():
        o_ref[0] = acc_ref[...].astype(o_ref.dtype)


def _dft_seq(dcat, ab, *, tm=1024, tk=1024):
    batch, two_seq, d_b = ab.shape
    seq = two_seq // 2
    return pl.pallas_call(
        _dft_seq_kernel,
        grid=(batch, seq // tm, two_seq // tk),
        in_specs=[pl.BlockSpec((tm, tk), lambda b, i, k: (i, k)),
                  pl.BlockSpec((1, tk, d_b), lambda b, i, k: (b, k, 0))],
        out_specs=pl.BlockSpec((1, tm, d_b), lambda b, i, k: (b, i, 0)),
        out_shape=jax.ShapeDtypeStruct((batch, seq, d_b), BF16),
        scratch_shapes=[pltpu.VMEM((tm, d_b), F32)],
        compiler_params=_cparams("parallel", "parallel", "arbitrary"),
        name="dft_sequence",
    )(dcat, ab)


def _dft_tables(seq, n):
    k = jnp.arange(seq, dtype=jnp.int32)
    ang = ((k[:, None] * k[None, :]) % seq).astype(F32) * (2.0 * math.pi / seq)
    dcat = (jnp.concatenate([jnp.cos(ang), -jnp.sin(ang)], axis=1) * (seq ** -0.5)).astype(BF16)
    l = jnp.arange(n, dtype=jnp.int32)
    angn = ((l[:, None] * l[None, :]) % n).astype(F32) * (2.0 * math.pi / n)
    cs = (jnp.concatenate([jnp.cos(angn), jnp.sin(angn)], axis=1) * (n ** -0.5)).astype(BF16)
    return dcat, cs


def _conf_kernel(a_ref, g_ref, ap_ref, gp_ref, an_ref, gn_ref, w_ref, cb_ref, lg_ref, lb_ref, o_ref, upad,
                 *, tps, row_chunk):
    tm = a_ref.shape[0]
    si = pl.program_id(0) % tps

    def glu(a, g):
        return a[...].astype(F32) * _sigmoid(g[...].astype(F32))

    upad[0:HALO, :] = jnp.where(si == 0, 0.0, glu(ap_ref, gp_ref))
    upad[HALO:HALO + tm, :] = glu(a_ref, g_ref)
    upad[HALO + tm:, :] = jnp.where(si == tps - 1, 0.0, glu(an_ref, gn_ref))

    half = CONF_CONV_W // 2
    for r0 in range(0, tm, row_chunk):
        acc = jnp.zeros((row_chunk, a_ref.shape[1]), F32) + cb_ref[...]
        for k in range(CONF_CONV_W):
            start = HALO + r0 + k - half
            acc = acc + w_ref[k:k + 1, :] * upad[start:start + row_chunk, :]
        mu = jnp.mean(acc, axis=-1, keepdims=True)
        cen = acc - mu
        var = jnp.mean(cen * cen, axis=-1, keepdims=True)
        y = cen * lax.rsqrt(var + NORM_EPS) * lg_ref[...] + lb_ref[...]
        o_ref[r0:r0 + row_chunk, :] = (y * _sigmoid(y)).astype(o_ref.dtype)


def _conf_conv(p, conv_w, conv_b, ln_g, ln_b, *, seq, d_c, tm=256, row_chunk=16):
    t = p.shape[0]
    tps = seq // tm
    hb = tm // HALO
    last_hb = t // HALO - 1
    main = lambda col: pl.BlockSpec((tm, d_c), lambda i: (i, col))
    prev = lambda col: pl.BlockSpec((HALO, d_c), lambda i: (jnp.maximum(i * hb - 1, 0), col))
    nxt = lambda col: pl.BlockSpec((HALO, d_c), lambda i: (jnp.minimum((i + 1) * hb, last_hb), col))
    vec = lambda rows: _resident((rows, d_c), lambda i: (0, 0))
    return pl.pallas_call(
        functools.partial(_conf_kernel, tps=tps, row_chunk=row_chunk),
        grid=(t // tm,),
        in_specs=[main(0), main(1), prev(0), prev(1), nxt(0), nxt(1),
                  vec(CONF_CONV_W), vec(1), vec(1), vec(1)],
        out_specs=pl.BlockSpec((tm, d_c), lambda i: (i, 0)),
        out_shape=jax.ShapeDtypeStruct((t, d_c), BF16),
        scratch_shapes=[pltpu.VMEM((tm + 2 * HALO, d_c), F32)],
        compiler_params=_cparams("parallel"),
        name="conformer_conv",
    )(p, p, p, p, p, p, conv_w, conv_b, ln_g, ln_b)


def _pool_kernel(z_ref, zp_ref, zn_ref, pw_ref, ps_ref, o_ref, zpad, *, tps, seq, row_chunk):
    tm = z_ref.shape[0]
    si = pl.program_id(0) % tps
    zpad[0:HALO, :] = jnp.where(si == 0, 0.0, zp_ref[...].astype(F32))
    zpad[HALO:HALO + tm, :] = z_ref[...].astype(F32)
    zpad[HALO + tm:, :] = jnp.where(si == tps - 1, 0.0, zn_ref[...].astype(F32))

    gw = z_ref.shape[1] // len(POOL_WINDOWS)
    for r0 in range(0, tm, row_chunk):
        t_seq = si * tm + r0 + lax.broadcasted_iota(jnp.int32, (row_chunk, 1), 0)
        for gi, w in enumerate(POOL_WINDOWS):
            left = w // 2
            right = w - 1 - left
            c0 = gi * gw
            tot = jnp.zeros((row_chunk, gw), F32)
            for dlt in range(-left, right + 1):
                start = HALO + r0 + dlt
                tot = tot + zpad[start:start + row_chunk, c0:c0 + gw]
            lo = jnp.maximum(t_seq - left, 0)
            hi = jnp.minimum(t_seq + right + 1, seq)
            cnt = (hi - lo).astype(F32)
            pooled = tot / cnt - zpad[HALO + r0:HALO + r0 + row_chunk, c0:c0 + gw]
            yd = jnp.dot(pooled.astype(BF16), pw_ref[gi], preferred_element_type=F32)
            o_ref[r0:r0 + row_chunk, c0:c0 + gw] = (yd * ps_ref[:, c0:c0 + gw]).astype(o_ref.dtype)


def _pool(p, pool_w, pool_scale, *, seq, d_d, col_block, tm=256, row_chunk=64):
    t = p.shape[0]
    tps = seq // tm
    hb = tm // HALO
    last_hb = t // HALO - 1
    n_pool, gw, _ = pool_w.shape
    return pl.pallas_call(
        functools.partial(_pool_kernel, tps=tps, seq=seq, row_chunk=row_chunk),
        grid=(t // tm,),
        in_specs=[pl.BlockSpec((tm, d_d), lambda i: (i, col_block)),
                  pl.BlockSpec((HALO, d_d), lambda i: (jnp.maximum(i * hb - 1, 0), col_block)),
                  pl.BlockSpec((HALO, d_d), lambda i: (jnp.minimum((i + 1) * hb, last_hb), col_block)),
                  _resident((n_pool, gw, gw), lambda i: (0, 0, 0)),
                  _resident((1, d_d), lambda i: (0, 0))],
        out_specs=pl.BlockSpec((tm, d_d), lambda i: (i, 0)),
        out_shape=jax.ShapeDtypeStruct((t, d_d), BF16),
        scratch_shapes=[pltpu.VMEM((tm + 2 * HALO, d_d), F32)],
        compiler_params=_cparams("parallel"),
        name="multiscale_pool",
    )(p, p, p, pool_w, pool_scale)


def _peerq_kernel(x_ref, g_ref, sc_ref, sh_ref, wq_ref, keys_ref, h_ref, sct_ref):
    h = _norm_mod(x_ref[...], g_ref[...], sc_ref[0], sh_ref[0]).astype(BF16)
    h_ref[...] = h
    n_keys, dhalf = keys_ref.shape[1], keys_ref.shape[2]
    for head in range(keys_ref.shape[0] // 2):
        c0 = head * 2 * dhalf
        q = jnp.dot(h, wq_ref[:, c0:c0 + 2 * dhalf], preferred_element_type=F32).astype(BF16)
        for half in range(2):
            sct_ref[2 * head + half] = lax.dot_general(
                keys_ref[2 * head + half], q[:, half * dhalf:(half + 1) * dhalf], NT_DIMS,
                preferred_element_type=F32)


def _peerq(x, g, sc, sh, wq, keys, *, seq, tm=512):
    t, d = x.shape
    n_hp, n_keys, dhalf = keys.shape
    tps = seq // tm
    return pl.pallas_call(
        _peerq_kernel,
        grid=(t // tm,),
        in_specs=[
            pl.BlockSpec((tm, d), lambda i: (i, 0)),
            _resident((1, d), lambda i: (0, 0)),
            pl.BlockSpec((1, 1, d), lambda i: (i // tps, 0, 0)),
            pl.BlockSpec((1, 1, d), lambda i: (i // tps, 0, 0)),
            _resident(wq.shape, lambda i: (0, 0)),
            _resident(keys.shape, lambda i: (0, 0, 0)),
        ],
        out_specs=[pl.BlockSpec((tm, d), lambda i: (i, 0)),
                   pl.BlockSpec((n_hp, n_keys, tm), lambda i: (0, 0, i))],
        out_shape=[jax.ShapeDtypeStruct((t, d), BF16),
                   jax.ShapeDtypeStruct((n_hp, n_keys, t), F32)],
        compiler_params=_cparams("parallel"),
        name="peer_queries",
    )(x, g, sc, sh, wq, keys)


def _top_values(x, k):
    vals = []
    for _ in range(k):
        m = jnp.max(x, axis=0, keepdims=True)
        vals.append(m)
        x = jnp.where(x >= m, -jnp.inf, x)
    return vals


def _stats_kernel(sct_ref, e2_ref, thr_ref, a0_ref):
    for head in range(e2_ref.shape[0]):
        s1 = sct_ref[2 * head]
        s2 = sct_ref[2 * head + 1]
        top1 = _top_values(s1, PEER_TOPK)
        top2 = jnp.concatenate(_top_values(s2, PEER_TOPK), axis=0)
        cand = jnp.concatenate([a + top2 for a in top1], axis=0)
        tops = _top_values(cand, PEER_TOPK)
        z = functools.reduce(lambda acc, v: acc + jnp.exp(v - tops[0]), tops[1:], jnp.ones_like(tops[0]))
        e2_ref[head] = jnp.exp(s2 - top2[0:1, :]) / z
        thr_ref[head:head + 1, :] = tops[PEER_TOPK - 1]
        a0_ref[head:head + 1, :] = top1[0]


def _stats(sct, *, tl=256):
    n_hp, n_keys, t = sct.shape
    heads = n_hp // 2
    return pl.pallas_call(
        _stats_kernel,
        grid=(t // tl,),
        in_specs=[pl.BlockSpec((n_hp, n_keys, tl), lambda i: (0, 0, i))],
        out_specs=[pl.BlockSpec((heads, n_keys, tl), lambda i: (0, 0, i)),
                   pl.BlockSpec((heads, tl), lambda i: (0, i)),
                   pl.BlockSpec((heads, tl), lambda i: (0, i))],
        out_shape=[jax.ShapeDtypeStruct((heads, n_keys, t), F32),
                   jax.ShapeDtypeStruct((heads, t), F32),
                   jax.ShapeDtypeStruct((heads, t), F32)],
        compiler_params=_cparams("parallel"),
        name="peer_gate_stats",
    )(sct)


def _dense_kernel(h_ref, u_ref, vt_ref, s1_ref, s2_ref, e2_ref, thr_ref, a0_ref, x_ref, gate_ref, fin_ref,
                  o_ref, acc_ref, a_ref, *, tc, final_norm):
    j = pl.program_id(1)
    heads, n_keys, tm = s2_ref.shape
    n1_per_step = s1_ref.shape[1]

    @pl.when(j == 0)
    def _():
        acc_ref[...] = jnp.zeros_like(acc_ref)

    h = h_ref[...]
    for c in range(n1_per_step):
        st = lax.dot_general(u_ref[c * n_keys:(c + 1) * n_keys, :], h, NT_DIMS, preferred_element_type=F32)
        for t0 in range(0, tm, tc):
            s = st[:, t0:t0 + tc]
            act = 0.5 * s * (1.0 + lax.erf(s * SQRT_HALF))
            w = jnp.zeros_like(s)
            for head in range(heads):
                s1_row = s1_ref[head, c:c + 1, t0:t0 + tc]
                e1_row = jnp.exp(s1_row - a0_ref[head:head + 1, t0:t0 + tc])
                ssum = s1_row + s2_ref[head, :, t0:t0 + tc]
                sel = ssum >= thr_ref[head:head + 1, t0:t0 + tc]
                w = w + jnp.where(sel, e2_ref[head, :, t0:t0 + tc], 0.0) * e1_row
            a_ref[c * n_keys:(c + 1) * n_keys, t0:t0 + tc] = (act * w).astype(a_ref.dtype)

    acc_ref[...] += jnp.dot(vt_ref[...], a_ref[...], preferred_element_type=F32)

    @pl.when(j == pl.num_programs(1) - 1)
    def _():
        d = acc_ref.shape[0]
        for d0 in range(0, d, tm):
            o_ref[:, d0:d0 + tm] = x_ref[:, d0:d0 + tm] + gate_ref[0][:, d0:d0 + tm] * acc_ref[d0:d0 + tm, :].T
        if final_norm:
            y = o_ref[...]
            r = lax.rsqrt(jnp.mean(y * y, axis=-1, keepdims=True) + NORM_EPS)
            o_ref[...] = (y * r) * fin_ref[...]


def _dense(h, u, vt, sct, e2, thr, a0, x, gate, fin, *, seq, final_norm, tm=512, tc=128):
    t, d = x.shape
    n_exp = u.shape[0]
    n_hp, n_keys, _ = sct.shape
    heads = n_hp // 2
    n1_per_step = 8
    te = n1_per_step * n_keys
    tps = seq // tm
    s12 = sct.reshape(heads, 2, n_keys, t)
    once = functools.partial(pl.BlockSpec, pipeline_mode=pl.Buffered(1))
    return pl.pallas_call(
        functools.partial(_dense_kernel, tc=tc, final_norm=final_norm),
        grid=(t // tm, n_exp // te),
        in_specs=[
            once((tm, d), lambda i, j: (i, 0)),
            pl.BlockSpec((te, d), lambda i, j: (j, 0)),
            pl.BlockSpec((d, te), lambda i, j: (0, j)),
            pl.BlockSpec((heads, None, n1_per_step, tm), lambda i, j: (0, 0, j, i)),
            once((heads, None, n_keys, tm), lambda i, j: (0, 1, 0, i)),
            once((heads, n_keys, tm), lambda i, j: (0, 0, i)),
            once((heads, tm), lambda i, j: (0, i)),
            once((heads, tm), lambda i, j: (0, i)),
            once((tm, d), lambda i, j: (i, 0)),
            pl.BlockSpec((1, 1, d), lambda i, j: (i // tps, 0, 0)),
            _resident((1, d), lambda i, j: (0, 0)),
        ],
        out_specs=pl.BlockSpec((tm, d), lambda i, j: (i, 0)),
        out_shape=jax.ShapeDtypeStruct((t, d), F32),
        scratch_shapes=[pltpu.VMEM((d, tm), F32), pltpu.VMEM((te, tm), BF16)],
        compiler_params=_cparams("parallel", "arbitrary"),
        name="peer_dense_experts",
    )(h, u, vt, s12, s12, e2, thr, a0, x, gate, fin)


def kernel(x_prompt, x_sample, c_prompt, c_sample, ada_w, ada_b, norm_mix, norm_ffn, ab_in_w, ab_conv_w, ab_conv_b, ab_out_w, cd_in_w, cd_in_b, cd_conv_w, cd_conv_b, cd_ln_g, cd_ln_b, cd_pool_w, cd_pool_scale, cd_out_w, peer_wq, peer_keys, peer_u, peer_v, final_norm):
    bp, seq, d = x_prompt.shape
    batch = bp + x_sample.shape[0]
    assert x_sample.shape[1] == seq
    t = batch * seq
    depth = ada_w.shape[0]
    d_a = ab_conv_w.shape[2]
    d_b = ab_in_w.shape[2] - 3 * d_a
    d_c = cd_conv_w.shape[2]
    d_d = cd_in_w.shape[2] - 2 * d_c
    assert d_a == d_b == d_c == d_d, "column blocks of the combined projections are addressed by block index"
    heads, _, n_keys, dhalf = peer_keys.shape[1:]

    x = jnp.concatenate([x_prompt, x_sample], axis=0).reshape(t, d)
    c = jnp.concatenate([c_prompt, c_sample], axis=0)
    c_pad = jnp.pad(c, ((0, -batch % 8), (0, 0)))
    mod = _modulation(c_pad, ada_w, ada_b)[:, :batch].reshape(depth, batch, 6, 1, d)

    dcat, cs = _dft_tables(seq, HEAD_DIM)
    row = lambda v: v.reshape(1, -1)

    for i in range(depth):
        sh_m, sc_m, g_m, sh_f, sc_f, g_f = [mod[i, :, k] for k in range(6)]
        j = i // 2
        if i % 2 == 0:
            w_in = ab_in_w[j].astype(BF16)
            p = _inproj(x, row(norm_mix[i]), sc_m, sh_m, w_in, jnp.zeros((1, w_in.shape[1]), F32), seq=seq)
            left = _gconv3(p, ab_conv_w[j], row(ab_conv_b[j]), seq=seq, d_a=d_a)
            ab = _dft_channel(p, cs, batch=batch, seq=seq, d_b=d_b, col_block=3)
            right = _dft_seq(dcat, ab.reshape(batch, 2 * seq, d_b)).reshape(t, d_b)
            w_out = ab_out_w[j].astype(BF16)
        else:
            p = _inproj(x, row(norm_mix[i]), sc_m, sh_m, cd_in_w[j].astype(BF16), row(cd_in_b[j]), seq=seq)
            left = _conf_conv(p, cd_conv_w[j], row(cd_conv_b[j]), row(cd_ln_g[j]), row(cd_ln_b[j]),
                              seq=seq, d_c=d_c)
            right = _pool(p, cd_pool_w[j].astype(BF16), row(cd_pool_scale[j]), seq=seq, d_d=d_d, col_block=2)
            w_out = cd_out_w[j].astype(BF16)
        x = _outproj(left, right, w_out, x, g_m, seq=seq)

        h, sct = _peerq(x, row(norm_ffn[i]), sc_f, sh_f, peer_wq[i].astype(BF16),
                        peer_keys[i].reshape(heads * 2, n_keys, dhalf).astype(BF16), seq=seq)
        e2, thr, a0 = _stats(sct)
        x = _dense(h, peer_u[i].astype(BF16), peer_v[i].T.astype(BF16), sct, e2, thr, a0, x, g_f,
                   row(final_norm), seq=seq, final_norm=(i == depth - 1))

    y = x.reshape(batch, seq, d)
    return (y[:bp], y[bp:])
```

```python
import functools
import math

import jax
import jax.numpy as jnp
from jax import lax
from jax.experimental import pallas as pl
from jax.experimental.pallas import tpu as pltpu

F32 = jnp.float32
BF16 = jnp.bfloat16

LANES = 128
BF16_SUBLANES = 16
VMEM_LIMIT_BYTES = 56 * 1024 * 1024

HEAD_DIM = 128
SHORT_CONV_W = 3
CONF_CONV_W = 31
POOL_WINDOWS = (2, 4, 8, 16)
PEER_TOPK = 16
NORM_EPS = 1e-6
SQRT_HALF = 0.7071067811865476

HALO = BF16_SUBLANES
assert CONF_CONV_W // 2 < HALO and max(POOL_WINDOWS) // 2 <= HALO

NT_DIMS = (((1,), (1,)), ((), ()))


def _cparams(*semantics, flags=None):
    return pltpu.CompilerParams(dimension_semantics=semantics, vmem_limit_bytes=VMEM_LIMIT_BYTES, flags=flags)


def _resident(block_shape, index_map):
    return pl.BlockSpec(block_shape, index_map, pipeline_mode=pl.Buffered(1))


def _sigmoid(x):
    return 1.0 / (1.0 + jnp.exp(-x))


def _norm_mod(x, g, sc, sh):
    r = lax.rsqrt(jnp.mean(x * x, axis=-1, keepdims=True) + NORM_EPS)
    return (x * r) * g * (1.0 + sc) + sh


def _mod_kernel(c_ref, w_ref, b_ref, o_ref):
    c = c_ref[...]
    a = (c * _sigmoid(c)).astype(BF16)
    o_ref[0] = jnp.dot(a, w_ref[0].astype(BF16), preferred_element_type=F32) + b_ref[0]


def _modulation(c_pad, ada_w, ada_b, *, tn=1024):
    depth, d, n = ada_w.shape
    rows = c_pad.shape[0]
    return pl.pallas_call(
        _mod_kernel,
        grid=(depth, n // tn),
        in_specs=[
            pl.BlockSpec((rows, d), lambda l, j: (0, 0)),
            pl.BlockSpec((1, d, tn), lambda l, j: (l, 0, j)),
            pl.BlockSpec((1, 1, tn), lambda l, j: (l, 0, j)),
        ],
        out_specs=pl.BlockSpec((1, rows, tn), lambda l, j: (l, 0, j)),
        out_shape=jax.ShapeDtypeStruct((depth, rows, n), F32),
        compiler_params=_cparams("parallel", "parallel"),
        name="adaln_modulation",
    )(c_pad, ada_w, ada_b.reshape(depth, 1, n))


def _inproj_kernel(x_ref, g_ref, sc_ref, sh_ref, w_ref, b_ref, o_ref, *, n_chunk):
    h = _norm_mod(x_ref[...], g_ref[...], sc_ref[0], sh_ref[0]).astype(BF16)
    for n0 in range(0, o_ref.shape[1], n_chunk):
        acc = jnp.dot(h, w_ref[:, n0:n0 + n_chunk], preferred_element_type=F32)
        o_ref[:, n0:n0 + n_chunk] = (acc + b_ref[:, n0:n0 + n_chunk]).astype(o_ref.dtype)


def _inproj(x, g, sc, sh, w, b, *, seq, tm=512, n_chunk=512):
    t, d = x.shape
    n = w.shape[1]
    tps = seq // tm
    return pl.pallas_call(
        functools.partial(_inproj_kernel, n_chunk=n_chunk),
        grid=(t // tm,),
        in_specs=[
            pl.BlockSpec((tm, d), lambda i: (i, 0)),
            _resident((1, d), lambda i: (0, 0)),
            pl.BlockSpec((1, 1, d), lambda i: (i // tps, 0, 0)),
            pl.BlockSpec((1, 1, d), lambda i: (i // tps, 0, 0)),
            _resident((d, n), lambda i: (0, 0)),
            _resident((1, n), lambda i: (0, 0)),
        ],
        out_specs=pl.BlockSpec((tm, n), lambda i: (i, 0)),
        out_shape=jax.ShapeDtypeStruct((t, n), BF16),
        compiler_params=_cparams("parallel"),
        name="norm_inproj",
    )(x, g, sc, sh, w, b)


def _outproj_kernel(l_ref, r_ref, w_ref, x_ref, gate_ref, o_ref, *, n_chunk):
    half = l_ref.shape[1]
    l = l_ref[...]
    r = r_ref[...]
    for n0 in range(0, o_ref.shape[1], n_chunk):
        acc = jnp.dot(l, w_ref[:half, n0:n0 + n_chunk], preferred_element_type=F32)
        acc += jnp.dot(r, w_ref[half:, n0:n0 + n_chunk], preferred_element_type=F32)
        o_ref[:, n0:n0 + n_chunk] = x_ref[:, n0:n0 + n_chunk] + gate_ref[0][:, n0:n0 + n_chunk] * acc


def _outproj(l, r, w, x, gate, *, seq, tm=512, n_chunk=512):
    t, d = x.shape
    half = l.shape[1]
    tps = seq // tm
    return pl.pallas_call(
        functools.partial(_outproj_kernel, n_chunk=n_chunk),
        grid=(t // tm,),
        in_specs=[
            pl.BlockSpec((tm, half), lambda i: (i, 0)),
            pl.BlockSpec((tm, half), lambda i: (i, 0)),
            _resident((2 * half, d), lambda i: (0, 0)),
            pl.BlockSpec((tm, d), lambda i: (i, 0)),
            pl.BlockSpec((1, 1, d), lambda i: (i // tps, 0, 0)),
        ],
        out_specs=pl.BlockSpec((tm, d), lambda i: (i, 0)),
        out_shape=jax.ShapeDtypeStruct((t, d), F32),
        compiler_params=_cparams("parallel"),
        name="outproj_residual",
    )(l, r, w, x, gate)


def _gconv3_kernel(gb_ref, gc_ref, v_ref, gcp_ref, vp_ref, gcn_ref, vn_ref, w_ref, b_ref, o_ref, *, tps):
    tm = gb_ref.shape[0]
    si = pl.program_id(0) % tps
    u = gc_ref[...].astype(F32) * v_ref[...].astype(F32)
    u_prev = (gcp_ref[...].astype(F32) * vp_ref[...].astype(F32))[HALO - 1:HALO, :]
    u_next = (gcn_ref[...].astype(F32) * vn_ref[...].astype(F32))[0:1, :]
    u_prev = jnp.where(si == 0, 0.0, u_prev)
    u_next = jnp.where(si == tps - 1, 0.0, u_next)
    row = lax.broadcasted_iota(jnp.int32, (tm, 1), 0)
    u_m1 = jnp.where(row == 0, u_prev, pltpu.roll(u, 1, 0))
    u_p1 = jnp.where(row == tm - 1, u_next, pltpu.roll(u, tm - 1, 0))
    y = w_ref[0:1, :] * u_m1 + w_ref[1:2, :] * u + w_ref[2:3, :] * u_p1 + b_ref[...]
    o_ref[...] = (gb_ref[...].astype(F32) * y).astype(o_ref.dtype)


def _gconv3(p, conv_w, conv_b, *, seq, d_a, tm=256):
    t = p.shape[0]
    tps = seq // tm
    hb = tm // HALO
    last_hb = t // HALO - 1
    main = lambda col: pl.BlockSpec((tm, d_a), lambda i: (i, col))
    prev = lambda col: pl.BlockSpec((HALO, d_a), lambda i: (jnp.maximum(i * hb - 1, 0), col))
    nxt = lambda col: pl.BlockSpec((HALO, d_a), lambda i: (jnp.minimum((i + 1) * hb, last_hb), col))
    return pl.pallas_call(
        functools.partial(_gconv3_kernel, tps=tps),
        grid=(t // tm,),
        in_specs=[main(0), main(1), main(2), prev(1), prev(2), nxt(1), nxt(2),
                  _resident((SHORT_CONV_W, d_a), lambda i: (0, 0)),
                  _resident((1, d_a), lambda i: (0, 0))],
        out_specs=pl.BlockSpec((tm, d_a), lambda i: (i, 0)),
        out_shape=jax.ShapeDtypeStruct((t, d_a), BF16),
        compiler_params=_cparams("parallel"),
        name="gated_conv3",
    )(p, p, p, p, p, p, p, conv_w, conv_b)


def _dft_channel_kernel(f_ref, cs_ref, o_ref):
    for g in range(f_ref.shape[1] // HEAD_DIM):
        c0 = g * HEAD_DIM
        r = jnp.dot(f_ref[:, c0:c0 + HEAD_DIM], cs_ref[...], preferred_element_type=F32)
        o_ref[0, 0, :, c0:c0 + HEAD_DIM] = r[:, :HEAD_DIM].astype(o_ref.dtype)
        o_ref[0, 1, :, c0:c0 + HEAD_DIM] = r[:, HEAD_DIM:].astype(o_ref.dtype)


def _dft_channel(p, cs, *, batch, seq, d_b, col_block, tm=512):
    tps = seq // tm
    return pl.pallas_call(
        _dft_channel_kernel,
        grid=(batch * tps,),
        in_specs=[pl.BlockSpec((tm, d_b), lambda i: (i, col_block)),
                  _resident((HEAD_DIM, 2 * HEAD_DIM), lambda i: (0, 0))],
        out_specs=pl.BlockSpec((1, 2, tm, d_b), lambda i: (i // tps, 0, i % tps, 0)),
        out_shape=jax.ShapeDtypeStruct((batch, 2, seq, d_b), BF16),
        compiler_params=_cparams("parallel"),
        name="dft_channels",
    )(p, cs)


def _dft_seq_kernel(d_ref, ab_ref, o_ref, acc_ref):
    k = pl.program_id(2)

    @pl.when(k == 0)
    def _():
        acc_ref[...] = jnp.zeros_like(acc_ref)

    acc_ref[...] += jnp.dot(d_ref[...], ab_ref[0], preferred_element_type=F32)

    @pl.when(k == pl.num_programs(2) - 1)
    def _():
        o_ref[0] = acc_ref[...].astype(o_ref.dtype)


def _dft_seq(dcat, ab, *, tm=1024, tk=1024):
    batch, two_seq, d_b = ab.shape
    seq = two_seq // 2
    return pl.pallas_call(
        _dft_seq_kernel,
        grid=(batch, seq // tm, two_seq // tk),
        in_specs=[pl.BlockSpec((tm, tk), lambda b, i, k: (i, k)),
                  pl.BlockSpec((1, tk, d_b), lambda b, i, k: (b, k, 0))],
        out_specs=pl.BlockSpec((1, tm, d_b), lambda b, i, k: (b, i, 0)),
        out_shape=jax.ShapeDtypeStruct((batch, seq, d_b), BF16),
        scratch_shapes=[pltpu.VMEM((tm, d_b), F32)],
        compiler_params=_cparams("parallel", "parallel", "arbitrary"),
        name="dft_sequence",
    )(dcat, ab)


def _dft_tables(seq, n):
    k = jnp.arange(seq, dtype=jnp.int32)
    ang = ((k[:, None] * k[None, :]) % seq).astype(F32) * (2.0 * math.pi / seq)
    dcat = (jnp.concatenate([jnp.cos(ang), -jnp.sin(ang)], axis=1) * (seq ** -0.5)).astype(BF16)
    l = jnp.arange(n, dtype=jnp.int32)
    angn = ((l[:, None] * l[None, :]) % n).astype(F32) * (2.0 * math.pi / n)
    cs = (jnp.concatenate([jnp.cos(angn), jnp.sin(angn)], axis=1) * (n ** -0.5)).astype(BF16)
    return dcat, cs


def _conf_kernel(a_ref, g_ref, ap_ref, gp_ref, an_ref, gn_ref, w_ref, cb_ref, lg_ref, lb_ref, o_ref, upad,
                 *, tps, row_chunk):
    tm = a_ref.shape[0]
    si = pl.program_id(0) % tps

    def glu(a, g):
        return a[...].astype(F32) * _sigmoid(g[...].astype(F32))

    upad[0:HALO, :] = jnp.where(si == 0, 0.0, glu(ap_ref, gp_ref))
    upad[HALO:HALO + tm, :] = glu(a_ref, g_ref)
    upad[HALO + tm:, :] = jnp.where(si == tps - 1, 0.0, glu(an_ref, gn_ref))

    half = CONF_CONV_W // 2
    for r0 in range(0, tm, row_chunk):
        acc = jnp.zeros((row_chunk, a_ref.shape[1]), F32) + cb_ref[...]
        for k in range(CONF_CONV_W):
            start = HALO + r0 + k - half
            acc = acc + w_ref[k:k + 1, :] * upad[start:start + row_chunk, :]
        mu = jnp.mean(acc, axis=-1, keepdims=True)
        cen = acc - mu
        var = jnp.mean(cen * cen, axis=-1, keepdims=True)
        y = cen * lax.rsqrt(var + NORM_EPS) * lg_ref[...] + lb_ref[...]
        o_ref[r0:r0 + row_chunk, :] = (y * _sigmoid(y)).astype(o_ref.dtype)


def _conf_conv(p, conv_w, conv_b, ln_g, ln_b, *, seq, d_c, tm=256, row_chunk=16):
    t = p.shape[0]
    tps = seq // tm
    hb = tm // HALO
    last_hb = t // HALO - 1
    main = lambda col: pl.BlockSpec((tm, d_c), lambda i: (i, col))
    prev = lambda col: pl.BlockSpec((HALO, d_c), lambda i: (jnp.maximum(i * hb - 1, 0), col))
    nxt = lambda col: pl.BlockSpec((HALO, d_c), lambda i: (jnp.minimum((i + 1) * hb, last_hb), col))
    vec = lambda rows: _resident((rows, d_c), lambda i: (0, 0))
    return pl.pallas_call(
        functools.partial(_conf_kernel, tps=tps, row_chunk=row_chunk),
        grid=(t // tm,),
        in_specs=[main(0), main(1), prev(0), prev(1), nxt(0), nxt(1),
                  vec(CONF_CONV_W), vec(1), vec(1), vec(1)],
        out_specs=pl.BlockSpec((tm, d_c), lambda i: (i, 0)),
        out_shape=jax.ShapeDtypeStruct((t, d_c), BF16),
        scratch_shapes=[pltpu.VMEM((tm + 2 * HALO, d_c), F32)],
        compiler_params=_cparams("parallel"),
        name="conformer_conv",
    )(p, p, p, p, p, p, conv_w, conv_b, ln_g, ln_b)


def _pool_kernel(z_ref, zp_ref, zn_ref, pw_ref, ps_ref, o_ref, zpad, *, tps, seq, row_chunk):
    tm = z_ref.shape[0]
    si = pl.program_id(0) % tps
    zpad[0:HALO, :] = jnp.where(si == 0, 0.0, zp_ref[...].astype(F32))
    zpad[HALO:HALO + tm, :] = z_ref[...].astype(F32)
    zpad[HALO + tm:, :] = jnp.where(si == tps - 1, 0.0, zn_ref[...].astype(F32))

    gw = z_ref.shape[1] // len(POOL_WINDOWS)
    for r0 in range(0, tm, row_chunk):
        t_seq = si * tm + r0 + lax.broadcasted_iota(jnp.int32, (row_chunk, 1), 0)
        for gi, w in enumerate(POOL_WINDOWS):
            left = w // 2
            right = w - 1 - left
            c0 = gi * gw
            tot = jnp.zeros((row_chunk, gw), F32)
            for dlt in range(-left, right + 1):
                start = HALO + r0 + dlt
                tot = tot + zpad[start:start + row_chunk, c0:c0 + gw]
            lo = jnp.maximum(t_seq - left, 0)
            hi = jnp.minimum(t_seq + right + 1, seq)
            cnt = (hi - lo).astype(F32)
            pooled = tot / cnt - zpad[HALO + r0:HALO + r0 + row_chunk, c0:c0 + gw]
            yd = jnp.dot(pooled.astype(BF16), pw_ref[gi], preferred_element_type=F32)
            o_ref[r0:r0 + row_chunk, c0:c0 + gw] = (yd * ps_ref[:, c0:c0 + gw]).astype(o_ref.dtype)


def _pool(p, pool_w, pool_scale, *, seq, d_d, col_block, tm=256, row_chunk=64):
    t = p.shape[0]
    tps = seq // tm
    hb = tm // HALO
    last_hb = t // HALO - 1
    n_pool, gw, _ = pool_w.shape
    return pl.pallas_call(
        functools.partial(_pool_kernel, tps=tps, seq=seq, row_chunk=row_chunk),
        grid=(t // tm,),
        in_specs=[pl.BlockSpec((tm, d_d), lambda i: (i, col_block)),
                  pl.BlockSpec((HALO, d_d), lambda i: (jnp.maximum(i * hb - 1, 0), col_block)),
                  pl.BlockSpec((HALO, d_d), lambda i: (jnp.minimum((i + 1) * hb, last_hb), col_block)),
                  _resident((n_pool, gw, gw), lambda i: (0, 0, 0)),
                  _resident((1, d_d), lambda i: (0, 0))],
        out_specs=pl.BlockSpec((tm, d_d), lambda i: (i, 0)),
        out_shape=jax.ShapeDtypeStruct((t, d_d), BF16),
        scratch_shapes=[pltpu.VMEM((tm + 2 * HALO, d_d), F32)],
        compiler_params=_cparams("parallel"),
        name="multiscale_pool",
    )(p, p, p, pool_w, pool_scale)


def _peerq_kernel(x_ref, g_ref, sc_ref, sh_ref, wq_ref, keys_ref, ht_ref, sct_ref):
    hf = _norm_mod(x_ref[...], g_ref[...], sc_ref[0], sh_ref[0])
    ht_ref[...] = hf.T.astype(BF16)
    h = hf.astype(BF16)
    n_keys, dhalf = keys_ref.shape[1], keys_ref.shape[2]
    for head in range(keys_ref.shape[0] // 2):
        c0 = head * 2 * dhalf
        q = jnp.dot(h, wq_ref[:, c0:c0 + 2 * dhalf], preferred_element_type=F32).astype(BF16)
        for half in range(2):
            sct_ref[2 * head + half] = lax.dot_general(
                keys_ref[2 * head + half], q[:, half * dhalf:(half + 1) * dhalf], NT_DIMS,
                preferred_element_type=F32)


def _peerq(x, g, sc, sh, wq, keys, *, seq, tm=512):
    t, d = x.shape
    n_hp, n_keys, dhalf = keys.shape
    tps = seq // tm
    return pl.pallas_call(
        _peerq_kernel,
        grid=(t // tm,),
        in_specs=[
            pl.BlockSpec((tm, d), lambda i: (i, 0)),
            _resident((1, d), lambda i: (0, 0)),
            pl.BlockSpec((1, 1, d), lambda i: (i // tps, 0, 0)),
            pl.BlockSpec((1, 1, d), lambda i: (i // tps, 0, 0)),
            _resident(wq.shape, lambda i: (0, 0)),
            _resident(keys.shape, lambda i: (0, 0, 0)),
        ],
        out_specs=[pl.BlockSpec((d, tm), lambda i: (0, i)),
                   pl.BlockSpec((n_hp, n_keys, tm), lambda i: (0, 0, i))],
        out_shape=[jax.ShapeDtypeStruct((d, t), BF16),
                   jax.ShapeDtypeStruct((n_hp, n_keys, t), F32)],
        compiler_params=_cparams("parallel"),
        name="peer_queries",
    )(x, g, sc, sh, wq, keys)


def _top_values(x, k):
    vals = []
    for _ in range(k):
        m = jnp.max(x, axis=0, keepdims=True)
        vals.append(m)
        x = jnp.where(x >= m, -jnp.inf, x)
    return vals


def _stats_kernel(sct_ref, e2_ref, thr_ref, a0_ref):
    for head in range(e2_ref.shape[0]):
        s1 = sct_ref[2 * head]
        s2 = sct_ref[2 * head + 1]
        top1 = _top_values(s1, PEER_TOPK)
        top2 = jnp.concatenate(_top_values(s2, PEER_TOPK), axis=0)
        cand = jnp.concatenate([a + top2 for a in top1], axis=0)
        tops = _top_values(cand, PEER_TOPK)
        z = functools.reduce(lambda acc, v: acc + jnp.exp(v - tops[0]), tops[1:], jnp.ones_like(tops[0]))
        e2_ref[head] = jnp.exp(s2 - top2[0:1, :]) / z
        thr_ref[head:head + 1, :] = tops[PEER_TOPK - 1]
        a0_ref[head:head + 1, :] = top1[0]


def _stats(sct, *, tl=256):
    n_hp, n_keys, t = sct.shape
    heads = n_hp // 2
    return pl.pallas_call(
        _stats_kernel,
        grid=(t // tl,),
        in_specs=[pl.BlockSpec((n_hp, n_keys, tl), lambda i: (0, 0, i))],
        out_specs=[pl.BlockSpec((heads, n_keys, tl), lambda i: (0, 0, i)),
                   pl.BlockSpec((heads, tl), lambda i: (0, i)),
                   pl.BlockSpec((heads, tl), lambda i: (0, i))],
        out_shape=[jax.ShapeDtypeStruct((heads, n_keys, t), F32),
                   jax.ShapeDtypeStruct((heads, t), F32),
                   jax.ShapeDtypeStruct((heads, t), F32)],
        compiler_params=_cparams("parallel"),
        name="peer_gate_stats",
    )(sct)


def _gated_activation(st_ref, a_ref, s1_ref, e1_ref, row0, t0, s2_ref, e2_ref, thr_ref, *, rows):
    heads, n_keys, _ = s2_ref.shape
    n1 = st_ref.shape[0] // n_keys
    lanes = pl.ds(t0, LANES)
    for r0 in range(0, n_keys, rows):
        w = [jnp.zeros((rows, LANES), F32) for _ in range(n1)]
        for head in range(heads):
            s2 = s2_ref[head, r0:r0 + rows, lanes]
            e2 = e2_ref[head, r0:r0 + rows, lanes]
            thr = thr_ref[head:head + 1, lanes]
            for k in range(n1):
                sel = (s1_ref[head, row0 + k:row0 + k + 1, lanes] + s2) >= thr
                w[k] = w[k] + jnp.where(sel, e2, 0.0) * e1_ref[head, row0 + k:row0 + k + 1, lanes]
        for k in range(n1):
            e0 = k * n_keys + r0
            s = st_ref[e0:e0 + rows, lanes]
            act = 0.5 * s * (1.0 + lax.erf(s * SQRT_HALF))
            a_ref[e0:e0 + rows, lanes] = (act * w[k]).astype(a_ref.dtype)


def _dense_kernel(ht_ref, u_ref, vt_ref, s1p_ref, s1c_ref, s2_ref, e2_ref, thr_ref, a0_ref, x_ref, gate_ref,
                  fin_ref, o_ref, acc_ref, st0_ref, st1_ref, a0s_ref, a1s_ref, e1p_ref, e1c_ref,
                  *, rows, final_norm):
    s = pl.program_id(1)
    heads, n_keys, tm = s2_ref.shape
    half = st0_ref.shape[0]
    d = acc_ref.shape[0]
    n1_half = s1c_ref.shape[1] // 2
    n_slices = tm // LANES
    u_rows = half // n_slices
    v_rows = d // n_slices

    @pl.when(s == 0)
    def _():
        acc_ref[...] = jnp.zeros_like(acc_ref)
        st1_ref[...] = jnp.zeros_like(st1_ref)
        a0s_ref[...] = jnp.zeros_like(a0s_ref)

    for head in range(heads):
        e1p_ref[head] = jnp.exp(s1p_ref[head] - a0_ref[head:head + 1, :])
        e1c_ref[head] = jnp.exp(s1c_ref[head] - a0_ref[head:head + 1, :])

    gates = functools.partial(_gated_activation, s2_ref=s2_ref, e2_ref=e2_ref, thr_ref=thr_ref, rows=rows)

    def half_block(u0, v0, st_out, st_in, a_out, a_in, s1_ref, e1_ref, row0):
        def body(p, carry):
            ur = pl.ds(pl.multiple_of(u0 + p * u_rows, u_rows), u_rows)
            st_out[pl.ds(pl.multiple_of(p * u_rows, u_rows), u_rows), :] = jnp.dot(
                u_ref[ur, :], ht_ref[...], preferred_element_type=F32)
            gates(st_in, a_out, s1_ref, e1_ref, row0, pl.multiple_of(p * LANES, LANES))
            vr = pl.ds(pl.multiple_of(p * v_rows, v_rows), v_rows)
            acc_ref[vr, :] += jnp.dot(vt_ref[vr, v0:v0 + half], a_in[...], preferred_element_type=F32)
            return carry
        lax.fori_loop(0, n_slices, body, 0)

    half_block(0, 0, st0_ref, st1_ref, a1s_ref, a0s_ref, s1p_ref, e1p_ref, n1_half)
    half_block(half, half, st1_ref, st0_ref, a0s_ref, a1s_ref, s1c_ref, e1c_ref, 0)

    @pl.when(s == pl.num_programs(1) - 1)
    def _():
        d = acc_ref.shape[0]
        for d0 in range(0, d, tm):
            o_ref[:, d0:d0 + tm] = x_ref[:, d0:d0 + tm] + gate_ref[0][:, d0:d0 + tm] * acc_ref[d0:d0 + tm, :].T
        if final_norm:
            y = o_ref[...]
            r = lax.rsqrt(jnp.mean(y * y, axis=-1, keepdims=True) + NORM_EPS)
            o_ref[...] = (y * r) * fin_ref[...]


def _dense(ht, u, vt, sct, e2, thr, a0, x, gate, fin, *, seq, final_norm, tm=512, rows=32):
    t, d = x.shape
    n_exp = u.shape[0]
    n_hp, n_keys, _ = sct.shape
    heads = n_hp // 2
    n1_per_step = 8
    te = n1_per_step * n_keys
    nj = n_exp // te
    tps = seq // tm
    s12 = sct.reshape(heads, 2, n_keys, t)
    once = functools.partial(pl.BlockSpec, pipeline_mode=pl.Buffered(1))
    prev = lambda s: jnp.maximum(s - 1, 0)
    return pl.pallas_call(
        functools.partial(_dense_kernel, rows=rows, final_norm=final_norm),
        grid=(t // tm, nj + 1),
        in_specs=[
            once((d, tm), lambda i, s: (0, i)),
            pl.BlockSpec((te, d), lambda i, s: (jnp.minimum(s, nj - 1), 0)),
            pl.BlockSpec((d, te), lambda i, s: (0, prev(s))),
            pl.BlockSpec((heads, None, n1_per_step, tm), lambda i, s: (0, 0, prev(s), i)),
            pl.BlockSpec((heads, None, n1_per_step, tm), lambda i, s: (0, 0, jnp.minimum(s, nj - 1), i)),
            once((heads, None, n_keys, tm), lambda i, j: (0, 1, 0, i)),
            once((heads, n_keys, tm), lambda i, j: (0, 0, i)),
            once((heads, tm), lambda i, j: (0, i)),
            once((heads, tm), lambda i, j: (0, i)),
            once((tm, d), lambda i, j: (i, 0)),
            pl.BlockSpec((1, 1, d), lambda i, j: (i // tps, 0, 0)),
            _resident((1, d), lambda i, j: (0, 0)),
        ],
        out_specs=pl.BlockSpec((tm, d), lambda i, j: (i, 0)),
        out_shape=jax.ShapeDtypeStruct((t, d), F32),
        scratch_shapes=[pltpu.VMEM((d, tm), F32),
                        pltpu.VMEM((te // 2, tm), F32), pltpu.VMEM((te // 2, tm), F32),
                        pltpu.VMEM((te // 2, tm), BF16), pltpu.VMEM((te // 2, tm), BF16),
                        pltpu.VMEM((heads, n1_per_step, tm), F32), pltpu.VMEM((heads, n1_per_step, tm), F32)],
        compiler_params=_cparams("parallel", "arbitrary"),
        name="peer_dense_experts",
    )(ht, u, vt, s12, s12, s12, e2, thr, a0, x, gate, fin)


def kernel(x_prompt, x_sample, c_prompt, c_sample, ada_w, ada_b, norm_mix, norm_ffn, ab_in_w, ab_conv_w, ab_conv_b, ab_out_w, cd_in_w, cd_in_b, cd_conv_w, cd_conv_b, cd_ln_g, cd_ln_b, cd_pool_w, cd_pool_scale, cd_out_w, peer_wq, peer_keys, peer_u, peer_v, final_norm):
    bp, seq, d = x_prompt.shape
    batch = bp + x_sample.shape[0]
    assert x_sample.shape[1] == seq
    t = batch * seq
    depth = ada_w.shape[0]
    d_a = ab_conv_w.shape[2]
    d_b = ab_in_w.shape[2] - 3 * d_a
    d_c = cd_conv_w.shape[2]
    d_d = cd_in_w.shape[2] - 2 * d_c
    assert d_a == d_b == d_c == d_d, "column blocks of the combined projections are addressed by block index"
    heads, _, n_keys, dhalf = peer_keys.shape[1:]

    x = jnp.concatenate([x_prompt, x_sample], axis=0).reshape(t, d)
    c = jnp.concatenate([c_prompt, c_sample], axis=0)
    c_pad = jnp.pad(c, ((0, -batch % 8), (0, 0)))
    mod = _modulation(c_pad, ada_w, ada_b)[:, :batch].reshape(depth, batch, 6, 1, d)

    dcat, cs = _dft_tables(seq, HEAD_DIM)
    row = lambda v: v.reshape(1, -1)

    for i in range(depth):
        sh_m, sc_m, g_m, sh_f, sc_f, g_f = [mod[i, :, k] for k in range(6)]
        j = i // 2
        if i % 2 == 0:
            w_in = ab_in_w[j].astype(BF16)
            p = _inproj(x, row(norm_mix[i]), sc_m, sh_m, w_in, jnp.zeros((1, w_in.shape[1]), F32), seq=seq)
            left = _gconv3(p, ab_conv_w[j], row(ab_conv_b[j]), seq=seq, d_a=d_a)
            ab = _dft_channel(p, cs, batch=batch, seq=seq, d_b=d_b, col_block=3)
            right = _dft_seq(dcat, ab.reshape(batch, 2 * seq, d_b)).reshape(t, d_b)
            w_out = ab_out_w[j].astype(BF16)
        else:
            p = _inproj(x, row(norm_mix[i]), sc_m, sh_m, cd_in_w[j].astype(BF16), row(cd_in_b[j]), seq=seq)
            left = _conf_conv(p, cd_conv_w[j], row(cd_conv_b[j]), row(cd_ln_g[j]), row(cd_ln_b[j]),
                              seq=seq, d_c=d_c)
            right = _pool(p, cd_pool_w[j].astype(BF16), row(cd_pool_scale[j]), seq=seq, d_d=d_d, col_block=2)
            w_out = cd_out_w[j].astype(BF16)
        x = _outproj(left, right, w_out, x, g_m, seq=seq)

        ht, sct = _peerq(x, row(norm_ffn[i]), sc_f, sh_f, peer_wq[i].astype(BF16),
                        peer_keys[i].reshape(heads * 2, n_keys, dhalf).astype(BF16), seq=seq)
        e2, thr, a0 = _stats(sct)
        x = _dense(ht, peer_u[i].astype(BF16), peer_v[i].T.astype(BF16), sct, e2, thr, a0, x, g_f,
                   row(final_norm), seq=seq, final_norm=(i == depth - 1))

    y = x.reshape(batch, seq, d)
    return (y[:bp], y[bp:])
```

```python
import functools
import math

import jax
import jax.numpy as jnp
from jax import lax
from jax.experimental import pallas as pl
from jax.experimental.pallas import tpu as pltpu

F32 = jnp.float32
BF16 = jnp.bfloat16

LANES = 128
BF16_SUBLANES = 16
VMEM_LIMIT_BYTES = 56 * 1024 * 1024

HEAD_DIM = 128
SHORT_CONV_W = 3
CONF_CONV_W = 31
POOL_WINDOWS = (2, 4, 8, 16)
PEER_TOPK = 16
NORM_EPS = 1e-6
SQRT_HALF = 0.7071067811865476

HALO = BF16_SUBLANES
assert CONF_CONV_W // 2 < HALO and max(POOL_WINDOWS) // 2 <= HALO

NT_DIMS = (((1,), (1,)), ((), ()))


def _cparams(*semantics, flags=None):
    return pltpu.CompilerParams(dimension_semantics=semantics, vmem_limit_bytes=VMEM_LIMIT_BYTES, flags=flags)


def _resident(block_shape, index_map):
    return pl.BlockSpec(block_shape, index_map, pipeline_mode=pl.Buffered(1))


def _sigmoid(x):
    return 1.0 / (1.0 + jnp.exp(-x))


def _norm_mod(x, g, sc, sh):
    r = lax.rsqrt(jnp.mean(x * x, axis=-1, keepdims=True) + NORM_EPS)
    return (x * r) * g * (1.0 + sc) + sh


def _mod_kernel(c_ref, w_ref, b_ref, o_ref):
    c = c_ref[...]
    a = (c * _sigmoid(c)).astype(BF16)
    o_ref[0] = jnp.dot(a, w_ref[0].astype(BF16), preferred_element_type=F32) + b_ref[0]


def _modulation(c_pad, ada_w, ada_b, *, tn=1024):
    depth, d, n = ada_w.shape
    rows = c_pad.shape[0]
    return pl.pallas_call(
        _mod_kernel,
        grid=(depth, n // tn),
        in_specs=[
            pl.BlockSpec((rows, d), lambda l, j: (0, 0)),
            pl.BlockSpec((1, d, tn), lambda l, j: (l, 0, j)),
            pl.BlockSpec((1, 1, tn), lambda l, j: (l, 0, j)),
        ],
        out_specs=pl.BlockSpec((1, rows, tn), lambda l, j: (l, 0, j)),
        out_shape=jax.ShapeDtypeStruct((depth, rows, n), F32),
        compiler_params=_cparams("parallel", "parallel"),
        name="adaln_modulation",
    )(c_pad, ada_w, ada_b.reshape(depth, 1, n))


def _row_parts(parts, tm):
    specs, starts, lo = [], [], 0
    for p in parts:
        n = p.shape[0] // tm
        specs.append(pl.BlockSpec((tm, p.shape[1]), lambda i, *_, lo=lo, n=n: (jnp.clip(i - lo, 0, n - 1), 0)))
        starts.append(lo)
        lo += n
    return specs, tuple(starts), lo


def _pick_part(refs, starts, cols=slice(None)):
    i = pl.program_id(0)
    x = refs[0][:, cols]
    for ref, lo in zip(refs[1:], starts[1:]):
        x = jnp.where(i >= lo, ref[:, cols], x)
    return x


def _inproj_kernel(*refs, n_chunk, starts):
    x_refs, (g_ref, sc_ref, sh_ref, w_ref, b_ref, o_ref) = refs[:len(starts)], refs[len(starts):]
    h = _norm_mod(_pick_part(x_refs, starts), g_ref[...], sc_ref[0], sh_ref[0]).astype(BF16)
    for n0 in range(0, o_ref.shape[1], n_chunk):
        acc = jnp.dot(h, w_ref[:, n0:n0 + n_chunk], preferred_element_type=F32)
        o_ref[:, n0:n0 + n_chunk] = (acc + b_ref[:, n0:n0 + n_chunk]).astype(o_ref.dtype)


def _inproj(x_parts, g, sc, sh, w, b, *, seq, tm=512, n_chunk=512):
    d = x_parts[0].shape[1]
    n = w.shape[1]
    tps = seq // tm
    x_specs, starts, n_tiles = _row_parts(x_parts, tm)
    t = n_tiles * tm
    return pl.pallas_call(
        functools.partial(_inproj_kernel, n_chunk=n_chunk, starts=starts),
        grid=(n_tiles,),
        in_specs=x_specs + [
            _resident((1, d), lambda i: (0, 0)),
            pl.BlockSpec((1, 1, d), lambda i: (i // tps, 0, 0)),
            pl.BlockSpec((1, 1, d), lambda i: (i // tps, 0, 0)),
            _resident((d, n), lambda i: (0, 0)),
            _resident((1, n), lambda i: (0, 0)),
        ],
        out_specs=pl.BlockSpec((tm, n), lambda i: (i, 0)),
        out_shape=jax.ShapeDtypeStruct((t, n), BF16),
        compiler_params=_cparams("parallel"),
        name="norm_inproj",
    )(*x_parts, g, sc, sh, w, b)


def _outproj_kernel(*refs, n_chunk, starts):
    x_refs, (l_ref, r_ref, w_ref, gate_ref, o_ref) = refs[:len(starts)], refs[len(starts):]
    half = l_ref.shape[1]
    l = l_ref[...]
    r = r_ref[...]
    for n0 in range(0, o_ref.shape[1], n_chunk):
        cols = slice(n0, n0 + n_chunk)
        acc = jnp.dot(l, w_ref[:half, cols], preferred_element_type=F32)
        acc += jnp.dot(r, w_ref[half:, cols], preferred_element_type=F32)
        o_ref[:, cols] = _pick_part(x_refs, starts, cols) + gate_ref[0][:, cols] * acc


def _outproj(l, r, w, x_parts, gate, *, seq, tm=512, n_chunk=512):
    d = x_parts[0].shape[1]
    half = l.shape[1]
    tps = seq // tm
    x_specs, starts, n_tiles = _row_parts(x_parts, tm)
    return pl.pallas_call(
        functools.partial(_outproj_kernel, n_chunk=n_chunk, starts=starts),
        grid=(n_tiles,),
        in_specs=x_specs + [
            pl.BlockSpec((tm, half), lambda i: (i, 0)),
            pl.BlockSpec((tm, half), lambda i: (i, 0)),
            _resident((2 * half, d), lambda i: (0, 0)),
            pl.BlockSpec((1, 1, d), lambda i: (i // tps, 0, 0)),
        ],
        out_specs=pl.BlockSpec((tm, d), lambda i: (i, 0)),
        out_shape=jax.ShapeDtypeStruct((n_tiles * tm, d), F32),
        compiler_params=_cparams("parallel"),
        name="outproj_residual",
    )(*x_parts, l, r, w, gate)


def _gconv3_kernel(gb_ref, gc_ref, v_ref, gcp_ref, vp_ref, gcn_ref, vn_ref, w_ref, b_ref, o_ref, *, tps):
    tm = gb_ref.shape[0]
    si = pl.program_id(0) % tps
    u = gc_ref[...].astype(F32) * v_ref[...].astype(F32)
    u_prev = (gcp_ref[...].astype(F32) * vp_ref[...].astype(F32))[HALO - 1:HALO, :]
    u_next = (gcn_ref[...].astype(F32) * vn_ref[...].astype(F32))[0:1, :]
    u_prev = jnp.where(si == 0, 0.0, u_prev)
    u_next = jnp.where(si == tps - 1, 0.0, u_next)
    row = lax.broadcasted_iota(jnp.int32, (tm, 1), 0)
    u_m1 = jnp.where(row == 0, u_prev, pltpu.roll(u, 1, 0))
    u_p1 = jnp.where(row == tm - 1, u_next, pltpu.roll(u, tm - 1, 0))
    y = w_ref[0:1, :] * u_m1 + w_ref[1:2, :] * u + w_ref[2:3, :] * u_p1 + b_ref[...]
    o_ref[...] = (gb_ref[...].astype(F32) * y).astype(o_ref.dtype)


def _gconv3(p, conv_w, conv_b, *, seq, d_a, tm=256):
    t = p.shape[0]
    tps = seq // tm
    hb = tm // HALO
    last_hb = t // HALO - 1
    main = lambda col: pl.BlockSpec((tm, d_a), lambda i: (i, col))
    prev = lambda col: pl.BlockSpec((HALO, d_a), lambda i: (jnp.maximum(i * hb - 1, 0), col))
    nxt = lambda col: pl.BlockSpec((HALO, d_a), lambda i: (jnp.minimum((i + 1) * hb, last_hb), col))
    return pl.pallas_call(
        functools.partial(_gconv3_kernel, tps=tps),
        grid=(t // tm,),
        in_specs=[main(0), main(1), main(2), prev(1), prev(2), nxt(1), nxt(2),
                  _resident((SHORT_CONV_W, d_a), lambda i: (0, 0)),
                  _resident((1, d_a), lambda i: (0, 0))],
        out_specs=pl.BlockSpec((tm, d_a), lambda i: (i, 0)),
        out_shape=jax.ShapeDtypeStruct((t, d_a), BF16),
        compiler_params=_cparams("parallel"),
        name="gated_conv3",
    )(p, p, p, p, p, p, p, conv_w, conv_b)


def _dft_channel_kernel(f_ref, cs_ref, o_ref):
    for g in range(f_ref.shape[1] // HEAD_DIM):
        c0 = g * HEAD_DIM
        r = jnp.dot(f_ref[:, c0:c0 + HEAD_DIM], cs_ref[...], preferred_element_type=F32)
        o_ref[0, 0, :, c0:c0 + HEAD_DIM] = r[:, :HEAD_DIM].astype(o_ref.dtype)
        o_ref[0, 1, :, c0:c0 + HEAD_DIM] = r[:, HEAD_DIM:].astype(o_ref.dtype)


def _dft_channel(p, cs, *, batch, seq, d_b, col_block, tm=512):
    tps = seq // tm
    return pl.pallas_call(
        _dft_channel_kernel,
        grid=(batch * tps,),
        in_specs=[pl.BlockSpec((tm, d_b), lambda i: (i, col_block)),
                  _resident((HEAD_DIM, 2 * HEAD_DIM), lambda i: (0, 0))],
        out_specs=pl.BlockSpec((1, 2, tm, d_b), lambda i: (i // tps, 0, i % tps, 0)),
        out_shape=jax.ShapeDtypeStruct((batch, 2, seq, d_b), BF16),
        compiler_params=_cparams("parallel"),
        name="dft_channels",
    )(p, cs)


def _dft_seq_kernel(d_ref, ab_ref, o_ref, acc_ref):
    k = pl.program_id(2)

    @pl.when(k == 0)
    def _():
        acc_ref[...] = jnp.zeros_like(acc_ref)

    acc_ref[...] += jnp.dot(d_ref[...], ab_ref[0], preferred_element_type=F32)

    @pl.when(k == pl.num_programs(2) - 1)
    def _():
        o_ref[0] = acc_ref[...].astype(o_ref.dtype)


def _dft_seq(dcat, ab, *, tm=1024, tk=1024):
    batch, two_seq, d_b = ab.shape
    seq = two_seq // 2
    return pl.pallas_call(
        _dft_seq_kernel,
        grid=(batch, seq // tm, two_seq // tk),
        in_specs=[pl.BlockSpec((tm, tk), lambda b, i, k: (i, k)),
                  pl.BlockSpec((1, tk, d_b), lambda b, i, k: (b, k, 0))],
        out_specs=pl.BlockSpec((1, tm, d_b), lambda b, i, k: (b, i, 0)),
        out_shape=jax.ShapeDtypeStruct((batch, seq, d_b), BF16),
        scratch_shapes=[pltpu.VMEM((tm, d_b), F32)],
        compiler_params=_cparams("parallel", "parallel", "arbitrary"),
        name="dft_sequence",
    )(dcat, ab)


def _dft_tables(seq, n):
    lo_n = 1 << (max(seq.bit_length() - 1, 0) // 2)
    assert seq % lo_n == 0
    s = jnp.arange(seq, dtype=jnp.int32)[None, :]
    theta = 2.0 * math.pi / seq
    hi = (jnp.arange(seq // lo_n, dtype=jnp.int32) * lo_n)[:, None]
    lo = jnp.arange(lo_n, dtype=jnp.int32)[:, None]
    ang_hi = ((hi * s) % seq).astype(F32) * theta
    ang_lo = ((lo * s) % seq).astype(F32) * theta
    ch, sh = jnp.cos(ang_hi)[:, None, :], jnp.sin(ang_hi)[:, None, :]
    cl, sl = jnp.cos(ang_lo)[None, :, :], jnp.sin(ang_lo)[None, :, :]
    cos_t = (ch * cl - sh * sl).reshape(seq, seq)
    sin_t = (sh * cl + ch * sl).reshape(seq, seq)
    dcat = (jnp.concatenate([cos_t, -sin_t], axis=1) * (seq ** -0.5)).astype(BF16)
    l = jnp.arange(n, dtype=jnp.int32)
    angn = ((l[:, None] * l[None, :]) % n).astype(F32) * (2.0 * math.pi / n)
    cs = (jnp.concatenate([jnp.cos(angn), jnp.sin(angn)], axis=1) * (n ** -0.5)).astype(BF16)
    return dcat, cs


def _conf_kernel(a_ref, g_ref, ap_ref, gp_ref, an_ref, gn_ref, w_ref, cb_ref, lg_ref, lb_ref, o_ref, upad,
                 *, tps, row_chunk):
    tm = a_ref.shape[0]
    si = pl.program_id(0) % tps

    def glu(a, g):
        return a[...].astype(F32) * _sigmoid(g[...].astype(F32))

    upad[0:HALO, :] = jnp.where(si == 0, 0.0, glu(ap_ref, gp_ref))
    upad[HALO:HALO + tm, :] = glu(a_ref, g_ref)
    upad[HALO + tm:, :] = jnp.where(si == tps - 1, 0.0, glu(an_ref, gn_ref))

    half = CONF_CONV_W // 2
    for r0 in range(0, tm, row_chunk):
        acc = jnp.zeros((row_chunk, a_ref.shape[1]), F32) + cb_ref[...]
        for k in range(CONF_CONV_W):
            start = HALO + r0 + k - half
            acc = acc + w_ref[k:k + 1, :] * upad[start:start + row_chunk, :]
        mu = jnp.mean(acc, axis=-1, keepdims=True)
        cen = acc - mu
        var = jnp.mean(cen * cen, axis=-1, keepdims=True)
        y = cen * lax.rsqrt(var + NORM_EPS) * lg_ref[...] + lb_ref[...]
        o_ref[r0:r0 + row_chunk, :] = (y * _sigmoid(y)).astype(o_ref.dtype)


def _conf_conv(p, conv_w, conv_b, ln_g, ln_b, *, seq, d_c, tm=256, row_chunk=16):
    t = p.shape[0]
    tps = seq // tm
    hb = tm // HALO
    last_hb = t // HALO - 1
    main = lambda col: pl.BlockSpec((tm, d_c), lambda i: (i, col))
    prev = lambda col: pl.BlockSpec((HALO, d_c), lambda i: (jnp.maximum(i * hb - 1, 0), col))
    nxt = lambda col: pl.BlockSpec((HALO, d_c), lambda i: (jnp.minimum((i + 1) * hb, last_hb), col))
    vec = lambda rows: _resident((rows, d_c), lambda i: (0, 0))
    return pl.pallas_call(
        functools.partial(_conf_kernel, tps=tps, row_chunk=row_chunk),
        grid=(t // tm,),
        in_specs=[main(0), main(1), prev(0), prev(1), nxt(0), nxt(1),
                  vec(CONF_CONV_W), vec(1), vec(1), vec(1)],
        out_specs=pl.BlockSpec((tm, d_c), lambda i: (i, 0)),
        out_shape=jax.ShapeDtypeStruct((t, d_c), BF16),
        scratch_shapes=[pltpu.VMEM((tm + 2 * HALO, d_c), F32)],
        compiler_params=_cparams("parallel"),
        name="conformer_conv",
    )(p, p, p, p, p, p, conv_w, conv_b, ln_g, ln_b)


def _pool_kernel(z_ref, zp_ref, zn_ref, pw_ref, ps_ref, o_ref, zpad, *, tps, seq, row_chunk):
    tm = z_ref.shape[0]
    si = pl.program_id(0) % tps
    zpad[0:HALO, :] = jnp.where(si == 0, 0.0, zp_ref[...].astype(F32))
    zpad[HALO:HALO + tm, :] = z_ref[...].astype(F32)
    zpad[HALO + tm:, :] = jnp.where(si == tps - 1, 0.0, zn_ref[...].astype(F32))

    gw = z_ref.shape[1] // len(POOL_WINDOWS)
    for r0 in range(0, tm, row_chunk):
        t_seq = si * tm + r0 + lax.broadcasted_iota(jnp.int32, (row_chunk, 1), 0)
        for gi, w in enumerate(POOL_WINDOWS):
            left = w // 2
            right = w - 1 - left
            c0 = gi * gw
            tot = jnp.zeros((row_chunk, gw), F32)
            for dlt in range(-left, right + 1):
                start = HALO + r0 + dlt
                tot = tot + zpad[start:start + row_chunk, c0:c0 + gw]
            lo = jnp.maximum(t_seq - left, 0)
            hi = jnp.minimum(t_seq + right + 1, seq)
            cnt = (hi - lo).astype(F32)
            pooled = tot / cnt - zpad[HALO + r0:HALO + r0 + row_chunk, c0:c0 + gw]
            yd = jnp.dot(pooled.astype(BF16), pw_ref[gi], preferred_element_type=F32)
            o_ref[r0:r0 + row_chunk, c0:c0 + gw] = (yd * ps_ref[:, c0:c0 + gw]).astype(o_ref.dtype)


def _pool(p, pool_w, pool_scale, *, seq, d_d, col_block, tm=256, row_chunk=64):
    t = p.shape[0]
    tps = seq // tm
    hb = tm // HALO
    last_hb = t // HALO - 1
    n_pool, gw, _ = pool_w.shape
    return pl.pallas_call(
        functools.partial(_pool_kernel, tps=tps, seq=seq, row_chunk=row_chunk),
        grid=(t // tm,),
        in_specs=[pl.BlockSpec((tm, d_d), lambda i: (i, col_block)),
                  pl.BlockSpec((HALO, d_d), lambda i: (jnp.maximum(i * hb - 1, 0), col_block)),
                  pl.BlockSpec((HALO, d_d), lambda i: (jnp.minimum((i + 1) * hb, last_hb), col_block)),
                  _resident((n_pool, gw, gw), lambda i: (0, 0, 0)),
                  _resident((1, d_d), lambda i: (0, 0))],
        out_specs=pl.BlockSpec((tm, d_d), lambda i: (i, 0)),
        out_shape=jax.ShapeDtypeStruct((t, d_d), BF16),
        scratch_shapes=[pltpu.VMEM((tm + 2 * HALO, d_d), F32)],
        compiler_params=_cparams("parallel"),
        name="multiscale_pool",
    )(p, p, p, pool_w, pool_scale)


def _peerq_kernel(x_ref, g_ref, sc_ref, sh_ref, wq_ref, keys_ref, ht_ref, sct_ref):
    hf = _norm_mod(x_ref[...], g_ref[...], sc_ref[0], sh_ref[0])
    ht_ref[...] = hf.T.astype(BF16)
    h = hf.astype(BF16)
    n_keys, dhalf = keys_ref.shape[1], keys_ref.shape[2]
    for head in range(keys_ref.shape[0] // 2):
        c0 = head * 2 * dhalf
        q = jnp.dot(h, wq_ref[:, c0:c0 + 2 * dhalf], preferred_element_type=F32).astype(BF16)
        for half in range(2):
            sct_ref[2 * head + half] = lax.dot_general(
                keys_ref[2 * head + half], q[:, half * dhalf:(half + 1) * dhalf], NT_DIMS,
                preferred_element_type=F32)


def _peerq(x, g, sc, sh, wq, keys, *, seq, tm=512):
    t, d = x.shape
    n_hp, n_keys, dhalf = keys.shape
    tps = seq // tm
    return pl.pallas_call(
        _peerq_kernel,
        grid=(t // tm,),
        in_specs=[
            pl.BlockSpec((tm, d), lambda i: (i, 0)),
            _resident((1, d), lambda i: (0, 0)),
            pl.BlockSpec((1, 1, d), lambda i: (i // tps, 0, 0)),
            pl.BlockSpec((1, 1, d), lambda i: (i // tps, 0, 0)),
            _resident(wq.shape, lambda i: (0, 0)),
            _resident(keys.shape, lambda i: (0, 0, 0)),
        ],
        out_specs=[pl.BlockSpec((d, tm), lambda i: (0, i)),
                   pl.BlockSpec((n_hp, n_keys, tm), lambda i: (0, 0, i))],
        out_shape=[jax.ShapeDtypeStruct((d, t), BF16),
                   jax.ShapeDtypeStruct((n_hp, n_keys, t), F32)],
        compiler_params=_cparams("parallel"),
        name="peer_queries",
    )(x, g, sc, sh, wq, keys)


F32_SUBLANES = 8


def _oddeven_mergesort_pairs(n):
    pairs = []
    p = 1
    while p < n:
        k = p
        while k >= 1:
            for j in range(k % p, n - k, 2 * k):
                for i in range(min(k, n - j - k)):
                    if (i + j) // (2 * p) == (i + j + k) // (2 * p):
                        pairs.append((i + j, i + j + k))
            k //= 2
        p *= 2
    return pairs


def _compare_exchange(x, i, j):
    x[i], x[j] = jnp.maximum(x[i], x[j]), jnp.minimum(x[i], x[j])


def _bitonic_sort_desc(x):
    dist = len(x) // 2
    while dist >= 1:
        for k in range(len(x)):
            if k & dist == 0:
                _compare_exchange(x, k, k + dist)
        dist //= 2


def _merge_top(x, y):
    n = len(x)
    z = [jnp.maximum(x[k], y[n - 1 - k]) if n - 1 - k < len(y) else x[k] for k in range(n)]
    _bitonic_sort_desc(z)
    return z


def _top_rows(s, k):
    x = [s[F32_SUBLANES * g:F32_SUBLANES * (g + 1), :] for g in range(k)]
    for i, j in _oddeven_mergesort_pairs(k):
        _compare_exchange(x, i, j)
    shift = F32_SUBLANES // 2
    while shift >= 1:
        x = _merge_top(x, [pltpu.roll(v, shift, 0) for v in x])
        shift //= 2
    return x


def _stats_kernel(sct_ref, e2_ref, thr_ref, a0_ref):
    heads, n_keys, tl = e2_ref.shape
    k = PEER_TOPK
    assert n_keys == k * F32_SUBLANES and heads == F32_SUBLANES
    sub = lax.broadcasted_iota(jnp.int32, (F32_SUBLANES, tl), 0)
    top1, top2, b0 = None, None, []
    for head in range(heads):
        t1 = _top_rows(sct_ref[2 * head], k)
        t2 = _top_rows(sct_ref[2 * head + 1], k)
        b0.append(t2[0])
        if head == 0:
            top1, top2 = t1, t2
        else:
            top1 = [jnp.where(sub == head, new, old) for new, old in zip(t1, top1)]
            top2 = [jnp.where(sub == head, new, old) for new, old in zip(t2, top2)]

    best = [top1[0] + top2[j] for j in range(k)]
    for i in range(1, k // 2):
        best = _merge_top(best, [top1[i] + top2[j] for j in range(k // (i + 1))])
    best = _merge_top(best, [top1[i] + top2[0] for i in range(k // 2, k)])
    z = functools.reduce(lambda acc, v: acc + jnp.exp(v - best[0]), best[1:], jnp.ones_like(best[0]))
    rz = 1.0 / z
    thr_ref[...] = best[k - 1]
    a0_ref[...] = top1[0]
    for head in range(heads):
        e2_ref[head] = jnp.exp(sct_ref[2 * head + 1] - b0[head][0:1, :]) * rz[head:head + 1, :]


def _stats(sct, *, tl=256):
    n_hp, n_keys, t = sct.shape
    heads = n_hp // 2
    return pl.pallas_call(
        _stats_kernel,
        grid=(t // tl,),
        in_specs=[pl.BlockSpec((n_hp, n_keys, tl), lambda i: (0, 0, i))],
        out_specs=[pl.BlockSpec((heads, n_keys, tl), lambda i: (0, 0, i)),
                   pl.BlockSpec((heads, tl), lambda i: (0, i)),
                   pl.BlockSpec((heads, tl), lambda i: (0, i))],
        out_shape=[jax.ShapeDtypeStruct((heads, n_keys, t), F32),
                   jax.ShapeDtypeStruct((heads, t), F32),
                   jax.ShapeDtypeStruct((heads, t), F32)],
        compiler_params=_cparams("parallel"),
        name="peer_gate_stats",
    )(sct)


def _gated_activation(st_ref, a_ref, s1_ref, e1_ref, row0, t0, s2_ref, e2_ref, thr_ref, *, rows):
    heads, n_keys, _ = s2_ref.shape
    n1 = st_ref.shape[0] // n_keys
    lanes = pl.ds(t0, LANES)
    for r0 in range(0, n_keys, rows):
        w = [jnp.zeros((rows, LANES), F32) for _ in range(n1)]
        for head in range(heads):
            s2 = s2_ref[head, r0:r0 + rows, lanes]
            e2 = e2_ref[head, r0:r0 + rows, lanes]
            thr = thr_ref[head:head + 1, lanes]
            for k in range(n1):
                sel = (s1_ref[head, row0 + k:row0 + k + 1, lanes] + s2) >= thr
                w[k] = w[k] + jnp.where(sel, e2, 0.0) * e1_ref[head, row0 + k:row0 + k + 1, lanes]
        for k in range(n1):
            e0 = k * n_keys + r0
            s = st_ref[e0:e0 + rows, lanes]
            act = 0.5 * s * (1.0 + lax.erf(s * SQRT_HALF))
            a_ref[e0:e0 + rows, lanes] = (act * w[k]).astype(a_ref.dtype)


def _dense_kernel(ht_ref, u_ref, vt_ref, s1p_ref, s1c_ref, s2_ref, e2_ref, thr_ref, a0_ref, x_ref, gate_ref,
                  fin_ref, *refs, rows, n_slices, final_norm, out_starts):
    o_refs = refs[:len(out_starts)]
    acc_ref, st0_ref, st1_ref, a0s_ref, a1s_ref, e1p_ref, e1c_ref = refs[len(out_starts):]
    s = pl.program_id(1)
    heads, _, tm = s2_ref.shape
    half = st0_ref.shape[0]
    d = acc_ref.shape[0]
    n1_half = s1c_ref.shape[1] // 2
    t_slice = tm // n_slices
    u_rows = half // n_slices
    v_rows = d // n_slices

    @pl.when(s == 0)
    def _():
        acc_ref[...] = jnp.zeros_like(acc_ref)
        st1_ref[...] = jnp.zeros_like(st1_ref)
        a0s_ref[...] = jnp.zeros_like(a0s_ref)

    for head in range(heads):
        e1p_ref[head] = jnp.exp(s1p_ref[head] - a0_ref[head:head + 1, :])
        e1c_ref[head] = jnp.exp(s1c_ref[head] - a0_ref[head:head + 1, :])

    gates = functools.partial(_gated_activation, s2_ref=s2_ref, e2_ref=e2_ref, thr_ref=thr_ref, rows=rows)

    def half_block(u0, v0, st_out, st_in, a_out, a_in, s1_ref, e1_ref, row0):
        def body(p, carry):
            ur = pl.ds(pl.multiple_of(u0 + p * u_rows, u_rows), u_rows)
            st_out[pl.ds(pl.multiple_of(p * u_rows, u_rows), u_rows), :] = jnp.dot(
                u_ref[ur, :], ht_ref[...], preferred_element_type=F32)
            vr = pl.ds(pl.multiple_of(p * v_rows, v_rows), v_rows)
            acc_ref[vr, :] += jnp.dot(vt_ref[vr, v0:v0 + half], a_in[...], preferred_element_type=F32)
            for c in range(0, t_slice, LANES):
                gates(st_in, a_out, s1_ref, e1_ref, row0, pl.multiple_of(p * t_slice + c, LANES))
            return carry
        lax.fori_loop(0, n_slices, body, 0)

    half_block(0, 0, st0_ref, st1_ref, a1s_ref, a0s_ref, s1p_ref, e1p_ref, n1_half)
    half_block(half, half, st1_ref, st0_ref, a0s_ref, a1s_ref, s1c_ref, e1c_ref, 0)

    def finish(o_ref):
        for d0 in range(0, d, tm):
            o_ref[:, d0:d0 + tm] = x_ref[:, d0:d0 + tm] + gate_ref[0][:, d0:d0 + tm] * acc_ref[d0:d0 + tm, :].T
        if final_norm:
            y = o_ref[...]
            r = lax.rsqrt(jnp.mean(y * y, axis=-1, keepdims=True) + NORM_EPS)
            o_ref[...] = (y * r) * fin_ref[...]

    i = pl.program_id(0)
    last = s == pl.num_programs(1) - 1
    bounds = out_starts[1:] + (pl.num_programs(0),)
    for o_ref, lo, hi in zip(o_refs, out_starts, bounds):
        pl.when(last & (i >= lo) & (i < hi))(functools.partial(finish, o_ref))


def _dense(ht, u, vt, sct, e2, thr, a0, x, gate, fin, *, seq, final_norm, out_rows, tm=512, rows=32,
           n_slices=2):
    t, d = x.shape
    assert sum(out_rows) == t and all(r % tm == 0 for r in out_rows)
    out_specs, out_starts, _ = _row_parts([jax.ShapeDtypeStruct((r, d), F32) for r in out_rows], tm)
    n_exp = u.shape[0]
    n_hp, n_keys, _ = sct.shape
    heads = n_hp // 2
    n1_per_step = 8
    te = n1_per_step * n_keys
    nj = n_exp // te
    tps = seq // tm
    s12 = sct.reshape(heads, 2, n_keys, t)
    once = functools.partial(pl.BlockSpec, pipeline_mode=pl.Buffered(1))
    prev = lambda s: jnp.maximum(s - 1, 0)
    cur = lambda s: jnp.minimum(s, nj - 1)
    return pl.pallas_call(
        functools.partial(_dense_kernel, rows=rows, n_slices=n_slices, final_norm=final_norm,
                          out_starts=out_starts),
        grid=(t // tm, nj + 1),
        in_specs=[
            once((d, tm), lambda i, s: (0, i)),
            pl.BlockSpec((te, d), lambda i, s: (cur(s), 0)),
            pl.BlockSpec((d, te), lambda i, s: (0, prev(s))),
            pl.BlockSpec((heads, None, n1_per_step, tm), lambda i, s: (0, 0, prev(s), i)),
            pl.BlockSpec((heads, None, n1_per_step, tm), lambda i, s: (0, 0, cur(s), i)),
            once((heads, None, n_keys, tm), lambda i, s: (0, 1, 0, i)),
            once((heads, n_keys, tm), lambda i, s: (0, 0, i)),
            once((heads, tm), lambda i, s: (0, i)),
            once((heads, tm), lambda i, s: (0, i)),
            once((tm, d), lambda i, s: (i, 0)),
            pl.BlockSpec((1, 1, d), lambda i, s: (i // tps, 0, 0)),
            _resident((1, d), lambda i, s: (0, 0)),
        ],
        out_specs=out_specs,
        out_shape=[jax.ShapeDtypeStruct((r, d), F32) for r in out_rows],
        scratch_shapes=[pltpu.VMEM((d, tm), F32),
                        pltpu.VMEM((te // 2, tm), F32), pltpu.VMEM((te // 2, tm), F32),
                        pltpu.VMEM((te // 2, tm), BF16), pltpu.VMEM((te // 2, tm), BF16),
                        pltpu.VMEM((heads, n1_per_step, tm), F32), pltpu.VMEM((heads, n1_per_step, tm), F32)],
        compiler_params=_cparams("arbitrary", "arbitrary"),
        name="peer_dense_experts",
    )(ht, u, vt, s12, s12, s12, e2, thr, a0, x, gate, fin)


def kernel(x_prompt, x_sample, c_prompt, c_sample, ada_w, ada_b, norm_mix, norm_ffn, ab_in_w, ab_conv_w, ab_conv_b, ab_out_w, cd_in_w, cd_in_b, cd_conv_w, cd_conv_b, cd_ln_g, cd_ln_b, cd_pool_w, cd_pool_scale, cd_out_w, peer_wq, peer_keys, peer_u, peer_v, final_norm):
    bp, seq, d = x_prompt.shape
    batch = bp + x_sample.shape[0]
    assert x_sample.shape[1] == seq
    t = batch * seq
    depth = ada_w.shape[0]
    d_a = ab_conv_w.shape[2]
    d_b = ab_in_w.shape[2] - 3 * d_a
    d_c = cd_conv_w.shape[2]
    d_d = cd_in_w.shape[2] - 2 * d_c
    assert d_a == d_b == d_c == d_d, "column blocks of the combined projections are addressed by block index"
    heads, _, n_keys, dhalf = peer_keys.shape[1:]

    group_rows = [bp * seq, (batch - bp) * seq]
    xs = [x_prompt.reshape(group_rows[0], d), x_sample.reshape(group_rows[1], d)]
    c = jnp.concatenate([c_prompt, c_sample], axis=0)
    c_pad = jnp.pad(c, ((0, -batch % 8), (0, 0)))
    mod = _modulation(c_pad, ada_w, ada_b)[:, :batch].reshape(depth, batch, 6, 1, d)

    dcat, cs = _dft_tables(seq, HEAD_DIM)
    row = lambda v: v.reshape(1, -1)

    for i in range(depth):
        sh_m, sc_m, g_m, sh_f, sc_f, g_f = [mod[i, :, k] for k in range(6)]
        j = i // 2
        if i % 2 == 0:
            w_in = ab_in_w[j].astype(BF16)
            p = _inproj(xs, row(norm_mix[i]), sc_m, sh_m, w_in, jnp.zeros((1, w_in.shape[1]), F32), seq=seq)
            left = _gconv3(p, ab_conv_w[j], row(ab_conv_b[j]), seq=seq, d_a=d_a)
            ab = _dft_channel(p, cs, batch=batch, seq=seq, d_b=d_b, col_block=3)
            right = _dft_seq(dcat, ab.reshape(batch, 2 * seq, d_b)).reshape(t, d_b)
            w_out = ab_out_w[j].astype(BF16)
        else:
            p = _inproj(xs, row(norm_mix[i]), sc_m, sh_m, cd_in_w[j].astype(BF16), row(cd_in_b[j]), seq=seq)
            left = _conf_conv(p, cd_conv_w[j], row(cd_conv_b[j]), row(cd_ln_g[j]), row(cd_ln_b[j]),
                              seq=seq, d_c=d_c)
            right = _pool(p, cd_pool_w[j].astype(BF16), row(cd_pool_scale[j]), seq=seq, d_d=d_d, col_block=2)
            w_out = cd_out_w[j].astype(BF16)
        x = _outproj(left, right, w_out, xs, g_m, seq=seq)

        ht, sct = _peerq(x, row(norm_ffn[i]), sc_f, sh_f, peer_wq[i].astype(BF16),
                        peer_keys[i].reshape(heads * 2, n_keys, dhalf).astype(BF16), seq=seq)
        e2, thr, a0 = _stats(sct)
        last = i == depth - 1
        xs = _dense(ht, peer_u[i].astype(BF16), peer_v[i].T.astype(BF16), sct, e2, thr, a0, x, g_f,
                    row(final_norm), seq=seq, final_norm=last, out_rows=group_rows if last else [t])

    return tuple(a.reshape(-1, seq, d) for a in xs)
```

```python
import functools
import math

import jax
import jax.numpy as jnp
from jax import lax
from jax.experimental import pallas as pl
from jax.experimental.pallas import tpu as pltpu

F32 = jnp.float32
BF16 = jnp.bfloat16

LANES = 128
F32_SUBLANES = 8
BF16_SUBLANES = 16
VMEM_LIMIT_BYTES = 56 * 1024 * 1024

HEAD_DIM = 128
SHORT_CONV_W = 3
CONF_CONV_W = 31
POOL_WINDOWS = (2, 4, 8, 16)
PEER_TOPK = 16
NORM_EPS = 1e-6
SQRT_HALF = 0.7071067811865476

HALO = BF16_SUBLANES
assert CONF_CONV_W // 2 < HALO and max(POOL_WINDOWS) // 2 <= HALO

NT_DIMS = (((1,), (1,)), ((), ()))


def _cparams(*semantics, flags=None):
    return pltpu.CompilerParams(dimension_semantics=semantics, vmem_limit_bytes=VMEM_LIMIT_BYTES, flags=flags)


def _resident(block_shape, index_map):
    return pl.BlockSpec(block_shape, index_map, pipeline_mode=pl.Buffered(1))


def _sigmoid(x):
    return 1.0 / (1.0 + jnp.exp(-x))


def _norm_mod(x, g, sc, sh):
    r = lax.rsqrt(jnp.mean(x * x, axis=-1, keepdims=True) + NORM_EPS)
    return (x * r) * g * (1.0 + sc) + sh


def _mod_kernel(c_ref, w_ref, b_ref, o_ref):
    c = c_ref[...]
    a = (c * _sigmoid(c)).astype(BF16)
    o_ref[0] = jnp.dot(a, w_ref[0].astype(BF16), preferred_element_type=F32) + b_ref[0]


def _modulation(c_pad, ada_w, ada_b, *, tn=1024):
    depth, d, n = ada_w.shape
    rows = c_pad.shape[0]
    return pl.pallas_call(
        _mod_kernel,
        grid=(depth, n // tn),
        in_specs=[
            pl.BlockSpec((rows, d), lambda l, j: (0, 0)),
            pl.BlockSpec((1, d, tn), lambda l, j: (l, 0, j)),
            pl.BlockSpec((1, 1, tn), lambda l, j: (l, 0, j)),
        ],
        out_specs=pl.BlockSpec((1, rows, tn), lambda l, j: (l, 0, j)),
        out_shape=jax.ShapeDtypeStruct((depth, rows, n), F32),
        compiler_params=_cparams("parallel", "parallel"),
        name="adaln_modulation",
    )(c_pad, ada_w, ada_b.reshape(depth, 1, n))


def _row_parts(parts, tm):
    specs, starts, lo = [], [], 0
    for p in parts:
        n = p.shape[0] // tm
        specs.append(pl.BlockSpec((tm, p.shape[1]), lambda i, *_, lo=lo, n=n: (jnp.clip(i - lo, 0, n - 1), 0)))
        starts.append(lo)
        lo += n
    return specs, tuple(starts), lo


def _pick_part(refs, starts, cols=slice(None)):
    i = pl.program_id(0)
    x = refs[0][:, cols]
    for ref, lo in zip(refs[1:], starts[1:]):
        x = jnp.where(i >= lo, ref[:, cols], x)
    return x


def _inproj_kernel(*refs, n_chunk, starts):
    x_refs, (g_ref, sc_ref, sh_ref, w_ref, b_ref, o_ref) = refs[:len(starts)], refs[len(starts):]
    h = _norm_mod(_pick_part(x_refs, starts), g_ref[...], sc_ref[0], sh_ref[0]).astype(BF16)
    for n0 in range(0, o_ref.shape[1], n_chunk):
        acc = jnp.dot(h, w_ref[:, n0:n0 + n_chunk], preferred_element_type=F32)
        o_ref[:, n0:n0 + n_chunk] = (acc + b_ref[:, n0:n0 + n_chunk]).astype(o_ref.dtype)


def _inproj(x_parts, g, sc, sh, w, b, *, seq, tm=512, n_chunk=512):
    d = x_parts[0].shape[1]
    n = w.shape[1]
    tps = seq // tm
    x_specs, starts, n_tiles = _row_parts(x_parts, tm)
    t = n_tiles * tm
    return pl.pallas_call(
        functools.partial(_inproj_kernel, n_chunk=n_chunk, starts=starts),
        grid=(n_tiles,),
        in_specs=x_specs + [
            _resident((1, d), lambda i: (0, 0)),
            pl.BlockSpec((1, 1, d), lambda i: (i // tps, 0, 0)),
            pl.BlockSpec((1, 1, d), lambda i: (i // tps, 0, 0)),
            _resident((d, n), lambda i: (0, 0)),
            _resident((1, n), lambda i: (0, 0)),
        ],
        out_specs=pl.BlockSpec((tm, n), lambda i: (i, 0)),
        out_shape=jax.ShapeDtypeStruct((t, n), BF16),
        compiler_params=_cparams("parallel"),
        name="norm_inproj",
    )(*x_parts, g, sc, sh, w, b)


def _outproj_kernel(*refs, n_chunk, starts):
    x_refs, (l_ref, r_ref, w_ref, gate_ref, o_ref) = refs[:len(starts)], refs[len(starts):]
    half = l_ref.shape[1]
    l = l_ref[...]
    r = r_ref[...]
    for n0 in range(0, o_ref.shape[1], n_chunk):
        cols = slice(n0, n0 + n_chunk)
        acc = jnp.dot(l, w_ref[:half, cols], preferred_element_type=F32)
        acc += jnp.dot(r, w_ref[half:, cols], preferred_element_type=F32)
        o_ref[:, cols] = _pick_part(x_refs, starts, cols) + gate_ref[0][:, cols] * acc


def _outproj(l, r, w, x_parts, gate, *, seq, tm=512, n_chunk=512):
    d = x_parts[0].shape[1]
    half = l.shape[1]
    tps = seq // tm
    x_specs, starts, n_tiles = _row_parts(x_parts, tm)
    return pl.pallas_call(
        functools.partial(_outproj_kernel, n_chunk=n_chunk, starts=starts),
        grid=(n_tiles,),
        in_specs=x_specs + [
            pl.BlockSpec((tm, half), lambda i: (i, 0)),
            pl.BlockSpec((tm, half), lambda i: (i, 0)),
            _resident((2 * half, d), lambda i: (0, 0)),
            pl.BlockSpec((1, 1, d), lambda i: (i // tps, 0, 0)),
        ],
        out_specs=pl.BlockSpec((tm, d), lambda i: (i, 0)),
        out_shape=jax.ShapeDtypeStruct((n_tiles * tm, d), F32),
        compiler_params=_cparams("parallel"),
        name="outproj_residual",
    )(*x_parts, l, r, w, gate)


def _gconv3_kernel(gb_ref, gc_ref, v_ref, gcp_ref, vp_ref, gcn_ref, vn_ref, w_ref, b_ref, o_ref, *, tps):
    tm = gb_ref.shape[0]
    si = pl.program_id(0) % tps
    u = gc_ref[...].astype(F32) * v_ref[...].astype(F32)
    u_prev = (gcp_ref[...].astype(F32) * vp_ref[...].astype(F32))[HALO - 1:HALO, :]
    u_next = (gcn_ref[...].astype(F32) * vn_ref[...].astype(F32))[0:1, :]
    u_prev = jnp.where(si == 0, 0.0, u_prev)
    u_next = jnp.where(si == tps - 1, 0.0, u_next)
    row = lax.broadcasted_iota(jnp.int32, (tm, 1), 0)
    u_m1 = jnp.where(row == 0, u_prev, pltpu.roll(u, 1, 0))
    u_p1 = jnp.where(row == tm - 1, u_next, pltpu.roll(u, tm - 1, 0))
    y = w_ref[0:1, :] * u_m1 + w_ref[1:2, :] * u + w_ref[2:3, :] * u_p1 + b_ref[...]
    o_ref[...] = (gb_ref[...].astype(F32) * y).astype(o_ref.dtype)


def _gconv3(p, conv_w, conv_b, *, seq, d_a, tm=256):
    t = p.shape[0]
    tps = seq // tm
    hb = tm // HALO
    last_hb = t // HALO - 1
    main = lambda col: pl.BlockSpec((tm, d_a), lambda i: (i, col))
    prev = lambda col: pl.BlockSpec((HALO, d_a), lambda i: (jnp.maximum(i * hb - 1, 0), col))
    nxt = lambda col: pl.BlockSpec((HALO, d_a), lambda i: (jnp.minimum((i + 1) * hb, last_hb), col))
    return pl.pallas_call(
        functools.partial(_gconv3_kernel, tps=tps),
        grid=(t // tm,),
        in_specs=[main(0), main(1), main(2), prev(1), prev(2), nxt(1), nxt(2),
                  _resident((SHORT_CONV_W, d_a), lambda i: (0, 0)),
                  _resident((1, d_a), lambda i: (0, 0))],
        out_specs=pl.BlockSpec((tm, d_a), lambda i: (i, 0)),
        out_shape=jax.ShapeDtypeStruct((t, d_a), BF16),
        compiler_params=_cparams("parallel"),
        name="gated_conv3",
    )(p, p, p, p, p, p, p, conv_w, conv_b)


def _dft_channel_kernel(f_ref, cs_ref, o_ref):
    for g in range(f_ref.shape[1] // HEAD_DIM):
        c0 = g * HEAD_DIM
        r = jnp.dot(f_ref[:, c0:c0 + HEAD_DIM], cs_ref[...], preferred_element_type=F32)
        o_ref[0, 0, :, c0:c0 + HEAD_DIM] = r[:, :HEAD_DIM].astype(o_ref.dtype)
        o_ref[0, 1, :, c0:c0 + HEAD_DIM] = r[:, HEAD_DIM:].astype(o_ref.dtype)


def _dft_channel(p, cs, *, batch, seq, d_b, col_block, tm=512):
    tps = seq // tm
    return pl.pallas_call(
        _dft_channel_kernel,
        grid=(batch * tps,),
        in_specs=[pl.BlockSpec((tm, d_b), lambda i: (i, col_block)),
                  _resident((HEAD_DIM, 2 * HEAD_DIM), lambda i: (0, 0))],
        out_specs=pl.BlockSpec((1, 2, tm, d_b), lambda i: (i // tps, 0, i % tps, 0)),
        out_shape=jax.ShapeDtypeStruct((batch, 2, seq, d_b), BF16),
        compiler_params=_cparams("parallel"),
        name="dft_channels",
    )(p, cs)


def _dft_seq_kernel(d_ref, ab_ref, o_ref, acc_ref):
    k = pl.program_id(2)

    @pl.when(k == 0)
    def _():
        acc_ref[...] = jnp.zeros_like(acc_ref)

    acc_ref[...] += jnp.dot(d_ref[...], ab_ref[0], preferred_element_type=F32)

    @pl.when(k == pl.num_programs(2) - 1)
    def _():
        o_ref[0] = acc_ref[...].astype(o_ref.dtype)


def _dft_seq(dcat, ab, *, tm=1024, tk=1024):
    batch, two_seq, d_b = ab.shape
    seq = two_seq // 2
    return pl.pallas_call(
        _dft_seq_kernel,
        grid=(batch, seq // tm, two_seq // tk),
        in_specs=[pl.BlockSpec((tm, tk), lambda b, i, k: (i, k)),
                  pl.BlockSpec((1, tk, d_b), lambda b, i, k: (b, k, 0))],
        out_specs=pl.BlockSpec((1, tm, d_b), lambda b, i, k: (b, i, 0)),
        out_shape=jax.ShapeDtypeStruct((batch, seq, d_b), BF16),
        scratch_shapes=[pltpu.VMEM((tm, d_b), F32)],
        compiler_params=_cparams("parallel", "parallel", "arbitrary"),
        name="dft_sequence",
    )(dcat, ab)


def _dft_tables(seq, n):
    lo_n = 1 << (max(seq.bit_length() - 1, 0) // 2)
    assert seq % lo_n == 0
    s = jnp.arange(seq, dtype=jnp.int32)[None, :]
    theta = 2.0 * math.pi / seq
    hi = (jnp.arange(seq // lo_n, dtype=jnp.int32) * lo_n)[:, None]
    lo = jnp.arange(lo_n, dtype=jnp.int32)[:, None]
    ang_hi = ((hi * s) % seq).astype(F32) * theta
    ang_lo = ((lo * s) % seq).astype(F32) * theta
    ch, sh = jnp.cos(ang_hi)[:, None, :], jnp.sin(ang_hi)[:, None, :]
    cl, sl = jnp.cos(ang_lo)[None, :, :], jnp.sin(ang_lo)[None, :, :]
    cos_t = (ch * cl - sh * sl).reshape(seq, seq)
    sin_t = (sh * cl + ch * sl).reshape(seq, seq)
    dcat = (jnp.concatenate([cos_t, -sin_t], axis=1) * (seq ** -0.5)).astype(BF16)
    l = jnp.arange(n, dtype=jnp.int32)
    angn = ((l[:, None] * l[None, :]) % n).astype(F32) * (2.0 * math.pi / n)
    cs = (jnp.concatenate([jnp.cos(angn), jnp.sin(angn)], axis=1) * (n ** -0.5)).astype(BF16)
    return dcat, cs


def _conf_kernel(a_ref, g_ref, ap_ref, gp_ref, an_ref, gn_ref, w_ref, cb_ref, lg_ref, lb_ref, o_ref, upad,
                 ushift, *, tps, row_chunk, copy_chunk):
    tm = a_ref.shape[0]
    si = pl.program_id(0) % tps

    def glu(a, g):
        return a[...].astype(F32) * _sigmoid(g[...].astype(F32))

    upad[0:HALO, :] = jnp.where(si == 0, 0.0, glu(ap_ref, gp_ref))
    upad[HALO:HALO + tm, :] = glu(a_ref, g_ref)
    upad[HALO + tm:, :] = jnp.where(si == tps - 1, 0.0, glu(an_ref, gn_ref))

    half = CONF_CONV_W // 2
    span = tm + (HALO + half) // F32_SUBLANES * F32_SUBLANES
    for r in range(1, F32_SUBLANES):
        for j0 in range(0, span, copy_chunk):
            ushift[r, j0:j0 + copy_chunk, :] = upad[j0 + r:j0 + r + copy_chunk, :]

    for r0 in range(0, tm, row_chunk):
        acc = jnp.zeros((row_chunk, a_ref.shape[1]), F32) + cb_ref[...]
        for k in range(CONF_CONV_W):
            q, r = divmod(HALO + k - half, F32_SUBLANES)
            start = r0 + q * F32_SUBLANES
            rows = upad[start:start + row_chunk, :] if r == 0 else ushift[r, start:start + row_chunk, :]
            acc = acc + w_ref[k:k + 1, :] * rows
        mu = jnp.mean(acc, axis=-1, keepdims=True)
        cen = acc - mu
        var = jnp.mean(cen * cen, axis=-1, keepdims=True)
        y = cen * lax.rsqrt(var + NORM_EPS) * lg_ref[...] + lb_ref[...]
        o_ref[r0:r0 + row_chunk, :] = (y * _sigmoid(y)).astype(o_ref.dtype)


def _conf_conv(p, conv_w, conv_b, ln_g, ln_b, *, seq, d_c, tm=256, row_chunk=16, copy_chunk=40):
    t = p.shape[0]
    tps = seq // tm
    hb = tm // HALO
    last_hb = t // HALO - 1
    main = lambda col: pl.BlockSpec((tm, d_c), lambda i: (i, col))
    prev = lambda col: pl.BlockSpec((HALO, d_c), lambda i: (jnp.maximum(i * hb - 1, 0), col))
    nxt = lambda col: pl.BlockSpec((HALO, d_c), lambda i: (jnp.minimum((i + 1) * hb, last_hb), col))
    vec = lambda rows: _resident((rows, d_c), lambda i: (0, 0))
    return pl.pallas_call(
        functools.partial(_conf_kernel, tps=tps, row_chunk=row_chunk, copy_chunk=copy_chunk),
        grid=(t // tm,),
        in_specs=[main(0), main(1), prev(0), prev(1), nxt(0), nxt(1),
                  vec(CONF_CONV_W), vec(1), vec(1), vec(1)],
        out_specs=pl.BlockSpec((tm, d_c), lambda i: (i, 0)),
        out_shape=jax.ShapeDtypeStruct((t, d_c), BF16),
        scratch_shapes=[pltpu.VMEM((tm + 2 * HALO, d_c), F32),
                        pltpu.VMEM((F32_SUBLANES, tm + 2 * HALO, d_c), F32)],
        compiler_params=_cparams("parallel"),
        name="conformer_conv",
    )(p, p, p, p, p, p, conv_w, conv_b, ln_g, ln_b)


def _pool_kernel(z_ref, zp_ref, zn_ref, pw_ref, ps_ref, o_ref, zpad, *, tps, seq, row_chunk):
    tm = z_ref.shape[0]
    si = pl.program_id(0) % tps
    zpad[0:HALO, :] = jnp.where(si == 0, 0.0, zp_ref[...].astype(F32))
    zpad[HALO:HALO + tm, :] = z_ref[...].astype(F32)
    zpad[HALO + tm:, :] = jnp.where(si == tps - 1, 0.0, zn_ref[...].astype(F32))

    gw = z_ref.shape[1] // len(POOL_WINDOWS)
    for r0 in range(0, tm, row_chunk):
        t_seq = si * tm + r0 + lax.broadcasted_iota(jnp.int32, (row_chunk, 1), 0)
        for gi, w in enumerate(POOL_WINDOWS):
            left = w // 2
            right = w - 1 - left
            c0 = gi * gw
            tot = jnp.zeros((row_chunk, gw), F32)
            for dlt in range(-left, right + 1):
                start = HALO + r0 + dlt
                tot = tot + zpad[start:start + row_chunk, c0:c0 + gw]
            lo = jnp.maximum(t_seq - left, 0)
            hi = jnp.minimum(t_seq + right + 1, seq)
            cnt = (hi - lo).astype(F32)
            pooled = tot / cnt - zpad[HALO + r0:HALO + r0 + row_chunk, c0:c0 + gw]
            yd = jnp.dot(pooled.astype(BF16), pw_ref[gi], preferred_element_type=F32)
            o_ref[r0:r0 + row_chunk, c0:c0 + gw] = (yd * ps_ref[:, c0:c0 + gw]).astype(o_ref.dtype)


def _pool(p, pool_w, pool_scale, *, seq, d_d, col_block, tm=256, row_chunk=64):
    t = p.shape[0]
    tps = seq // tm
    hb = tm // HALO
    last_hb = t // HALO - 1
    n_pool, gw, _ = pool_w.shape
    return pl.pallas_call(
        functools.partial(_pool_kernel, tps=tps, seq=seq, row_chunk=row_chunk),
        grid=(t // tm,),
        in_specs=[pl.BlockSpec((tm, d_d), lambda i: (i, col_block)),
                  pl.BlockSpec((HALO, d_d), lambda i: (jnp.maximum(i * hb - 1, 0), col_block)),
                  pl.BlockSpec((HALO, d_d), lambda i: (jnp.minimum((i + 1) * hb, last_hb), col_block)),
                  _resident((n_pool, gw, gw), lambda i: (0, 0, 0)),
                  _resident((1, d_d), lambda i: (0, 0))],
        out_specs=pl.BlockSpec((tm, d_d), lambda i: (i, 0)),
        out_shape=jax.ShapeDtypeStruct((t, d_d), BF16),
        scratch_shapes=[pltpu.VMEM((tm + 2 * HALO, d_d), F32)],
        compiler_params=_cparams("parallel"),
        name="multiscale_pool",
    )(p, p, p, pool_w, pool_scale)


def _fold_keys_kernel(keys_ref, wq_ref, o_ref):
    o_ref[0] = lax.dot_general(keys_ref[0], wq_ref[...], NT_DIMS, precision=lax.Precision.HIGHEST,
                               preferred_element_type=F32).astype(o_ref.dtype)


def _fold_keys(keys, wq):
    n_hp, n_keys, dhalf = keys.shape
    d = wq.shape[0]
    wk = pl.pallas_call(
        _fold_keys_kernel,
        grid=(n_hp,),
        in_specs=[pl.BlockSpec((1, n_keys, dhalf), lambda j: (j, 0, 0)),
                  pl.BlockSpec((d, dhalf), lambda j: (0, j))],
        out_specs=pl.BlockSpec((1, n_keys, d), lambda j: (j, 0, 0)),
        out_shape=jax.ShapeDtypeStruct((n_hp, n_keys, d), BF16),
        compiler_params=_cparams("parallel"),
        name="peer_fold_keys",
    )(keys, wq)
    return wk.reshape(n_hp * n_keys, d)


def _peerq_kernel(x_ref, g_ref, sc_ref, sh_ref, wk_ref, ht_ref, sct_ref, *, m_chunk):
    hf = _norm_mod(x_ref[...], g_ref[...], sc_ref[0], sh_ref[0])
    ht = hf.T.astype(BF16)
    ht_ref[...] = ht
    n_hp, n_keys, _ = sct_ref.shape
    per_chunk = m_chunk // n_keys
    for hp0 in range(0, n_hp, per_chunk):
        res = jnp.dot(wk_ref[hp0 * n_keys:(hp0 + per_chunk) * n_keys, :], ht, preferred_element_type=F32)
        for c in range(per_chunk):
            sct_ref[hp0 + c] = res[c * n_keys:(c + 1) * n_keys, :]


def _peerq(x, g, sc, sh, wk, *, n_keys, seq, tm=512, m_chunk=512):
    t, d = x.shape
    n_hp = wk.shape[0] // n_keys
    tps = seq // tm
    return pl.pallas_call(
        functools.partial(_peerq_kernel, m_chunk=m_chunk),
        grid=(t // tm,),
        in_specs=[
            pl.BlockSpec((tm, d), lambda i: (i, 0)),
            _resident((1, d), lambda i: (0, 0)),
            pl.BlockSpec((1, 1, d), lambda i: (i // tps, 0, 0)),
            pl.BlockSpec((1, 1, d), lambda i: (i // tps, 0, 0)),
            _resident(wk.shape, lambda i: (0, 0)),
        ],
        out_specs=[pl.BlockSpec((d, tm), lambda i: (0, i)),
                   pl.BlockSpec((n_hp, n_keys, tm), lambda i: (0, 0, i))],
        out_shape=[jax.ShapeDtypeStruct((d, t), BF16),
                   jax.ShapeDtypeStruct((n_hp, n_keys, t), F32)],
        compiler_params=_cparams("parallel"),
        name="peer_queries",
    )(x, g, sc, sh, wk)


def _oddeven_mergesort_pairs(n):
    pairs = []
    p = 1
    while p < n:
        k = p
        while k >= 1:
            for j in range(k % p, n - k, 2 * k):
                for i in range(min(k, n - j - k)):
                    if (i + j) // (2 * p) == (i + j + k) // (2 * p):
                        pairs.append((i + j, i + j + k))
            k //= 2
        p *= 2
    return pairs


def _compare_exchange(x, i, j):
    x[i], x[j] = jnp.maximum(x[i], x[j]), jnp.minimum(x[i], x[j])


def _bitonic_sort_desc(x):
    dist = len(x) // 2
    while dist >= 1:
        for k in range(len(x)):
            if k & dist == 0:
                _compare_exchange(x, k, k + dist)
        dist //= 2


def _merge_top(x, y):
    n = len(x)
    z = [jnp.maximum(x[k], y[n - 1 - k]) if n - 1 - k < len(y) else x[k] for k in range(n)]
    _bitonic_sort_desc(z)
    return z


def _top_rows(s, k):
    x = [s[F32_SUBLANES * g:F32_SUBLANES * (g + 1), :] for g in range(k)]
    for i, j in _oddeven_mergesort_pairs(k):
        _compare_exchange(x, i, j)
    shift = F32_SUBLANES // 2
    while shift >= 1:
        x = _merge_top(x, [pltpu.roll(v, shift, 0) for v in x])
        shift //= 2
    return x


def _stats_kernel(sct_ref, e2_ref, thr_ref, a0_ref):
    heads, n_keys, tl = e2_ref.shape
    k = PEER_TOPK
    assert n_keys == k * F32_SUBLANES and heads == F32_SUBLANES
    sub = lax.broadcasted_iota(jnp.int32, (F32_SUBLANES, tl), 0)
    top1, top2, b0 = None, None, []
    for head in range(heads):
        t1 = _top_rows(sct_ref[2 * head], k)
        t2 = _top_rows(sct_ref[2 * head + 1], k)
        b0.append(t2[0])
        if head == 0:
            top1, top2 = t1, t2
        else:
            top1 = [jnp.where(sub == head, new, old) for new, old in zip(t1, top1)]
            top2 = [jnp.where(sub == head, new, old) for new, old in zip(t2, top2)]

    best = [top1[0] + top2[j] for j in range(k)]
    for i in range(1, k // 2):
        best = _merge_top(best, [top1[i] + top2[j] for j in range(k // (i + 1))])
    best = _merge_top(best, [top1[i] + top2[0] for i in range(k // 2, k)])
    z = functools.reduce(lambda acc, v: acc + jnp.exp(v - best[0]), best[1:], jnp.ones_like(best[0]))
    rz = 1.0 / z
    thr_ref[...] = best[k - 1]
    a0_ref[...] = top1[0]
    for head in range(heads):
        e2_ref[head] = jnp.exp(sct_ref[2 * head + 1] - b0[head][0:1, :]) * rz[head:head + 1, :]


def _stats(sct, *, tl=256):
    n_hp, n_keys, t = sct.shape
    heads = n_hp // 2
    return pl.pallas_call(
        _stats_kernel,
        grid=(t // tl,),
        in_specs=[pl.BlockSpec((n_hp, n_keys, tl), lambda i: (0, 0, i))],
        out_specs=[pl.BlockSpec((heads, n_keys, tl), lambda i: (0, 0, i)),
                   pl.BlockSpec((heads, tl), lambda i: (0, i)),
                   pl.BlockSpec((heads, tl), lambda i: (0, i))],
        out_shape=[jax.ShapeDtypeStruct((heads, n_keys, t), F32),
                   jax.ShapeDtypeStruct((heads, t), F32),
                   jax.ShapeDtypeStruct((heads, t), F32)],
        compiler_params=_cparams("parallel"),
        name="peer_gate_stats",
    )(sct)


def _gated_activation(st_ref, a_ref, s1_ref, e1_ref, row0, t0, s2_ref, e2_ref, thr_ref, *, rows):
    heads, n_keys, _ = s2_ref.shape
    n1 = st_ref.shape[0] // n_keys
    lanes = pl.ds(t0, LANES)
    for r0 in range(0, n_keys, rows):
        w = [jnp.zeros((rows, LANES), F32) for _ in range(n1)]
        for head in range(heads):
            s2 = s2_ref[head, r0:r0 + rows, lanes]
            e2 = e2_ref[head, r0:r0 + rows, lanes]
            thr = thr_ref[head:head + 1, lanes]
            for k in range(n1):
                sel = (s1_ref[head, row0 + k:row0 + k + 1, lanes] + s2) >= thr
                w[k] = w[k] + jnp.where(sel, e2, 0.0) * e1_ref[head, row0 + k:row0 + k + 1, lanes]
        for k in range(n1):
            e0 = k * n_keys + r0
            s = st_ref[e0:e0 + rows, lanes]
            act = 0.5 * s * (1.0 + lax.erf(s * SQRT_HALF))
            a_ref[e0:e0 + rows, lanes] = (act * w[k]).astype(a_ref.dtype)


def _dense_kernel(ht_ref, u_ref, vt_ref, s1p_ref, s1c_ref, s2_ref, e2_ref, thr_ref, a0_ref, x_ref, gate_ref,
                  fin_ref, *refs, rows, n_slices, final_norm, out_starts):
    o_refs = refs[:len(out_starts)]
    acc_ref, st0_ref, st1_ref, a0s_ref, a1s_ref, e1p_ref, e1c_ref = refs[len(out_starts):]
    s = pl.program_id(1)
    heads, _, tm = s2_ref.shape
    half = st0_ref.shape[0]
    d = acc_ref.shape[0]
    n1_half = s1c_ref.shape[1] // 2
    t_slice = tm // n_slices
    u_rows = half // n_slices
    v_rows = d // n_slices

    @pl.when(s == 0)
    def _():
        acc_ref[...] = jnp.zeros_like(acc_ref)
        st1_ref[...] = jnp.zeros_like(st1_ref)
        a0s_ref[...] = jnp.zeros_like(a0s_ref)

    for head in range(heads):
        e1p_ref[head] = jnp.exp(s1p_ref[head] - a0_ref[head:head + 1, :])
        e1c_ref[head] = jnp.exp(s1c_ref[head] - a0_ref[head:head + 1, :])

    gates = functools.partial(_gated_activation, s2_ref=s2_ref, e2_ref=e2_ref, thr_ref=thr_ref, rows=rows)

    def half_block(u0, v0, st_out, st_in, a_out, a_in, s1_ref, e1_ref, row0):
        def body(p, carry):
            ur = pl.ds(pl.multiple_of(u0 + p * u_rows, u_rows), u_rows)
            st_out[pl.ds(pl.multiple_of(p * u_rows, u_rows), u_rows), :] = jnp.dot(
                u_ref[ur, :], ht_ref[...], preferred_element_type=F32)
            vr = pl.ds(pl.multiple_of(p * v_rows, v_rows), v_rows)
            acc_ref[vr, :] += jnp.dot(vt_ref[vr, v0:v0 + half], a_in[...], preferred_element_type=F32)
            for c in range(0, t_slice, LANES):
                gates(st_in, a_out, s1_ref, e1_ref, row0, pl.multiple_of(p * t_slice + c, LANES))
            return carry
        lax.fori_loop(0, n_slices, body, 0)

    half_block(0, 0, st0_ref, st1_ref, a1s_ref, a0s_ref, s1p_ref, e1p_ref, n1_half)
    half_block(half, half, st1_ref, st0_ref, a0s_ref, a1s_ref, s1c_ref, e1c_ref, 0)

    def finish(o_ref):
        for d0 in range(0, d, tm):
            o_ref[:, d0:d0 + tm] = x_ref[:, d0:d0 + tm] + gate_ref[0][:, d0:d0 + tm] * acc_ref[d0:d0 + tm, :].T
        if final_norm:
            y = o_ref[...]
            r = lax.rsqrt(jnp.mean(y * y, axis=-1, keepdims=True) + NORM_EPS)
            o_ref[...] = (y * r) * fin_ref[...]

    i = pl.program_id(0)
    last = s == pl.num_programs(1) - 1
    bounds = out_starts[1:] + (pl.num_programs(0),)
    for o_ref, lo, hi in zip(o_refs, out_starts, bounds):
        pl.when(last & (i >= lo) & (i < hi))(functools.partial(finish, o_ref))


def _dense(ht, u, vt, sct, e2, thr, a0, x, gate, fin, *, seq, final_norm, out_rows, tm=512, rows=32,
           n_slices=2):
    t, d = x.shape
    assert sum(out_rows) == t and all(r % tm == 0 for r in out_rows)
    out_specs, out_starts, _ = _row_parts([jax.ShapeDtypeStruct((r, d), F32) for r in out_rows], tm)
    n_exp = u.shape[0]
    n_hp, n_keys, _ = sct.shape
    heads = n_hp // 2
    n1_per_step = 8
    te = n1_per_step * n_keys
    nj = n_exp // te
    tps = seq // tm
    s12 = sct.reshape(heads, 2, n_keys, t)
    once = functools.partial(pl.BlockSpec, pipeline_mode=pl.Buffered(1))
    prev = lambda s: jnp.maximum(s - 1, 0)
    cur = lambda s: jnp.minimum(s, nj - 1)
    return pl.pallas_call(
        functools.partial(_dense_kernel, rows=rows, n_slices=n_slices, final_norm=final_norm,
                          out_starts=out_starts),
        grid=(t // tm, nj + 1),
        in_specs=[
            once((d, tm), lambda i, s: (0, i)),
            pl.BlockSpec((te, d), lambda i, s: (cur(s), 0)),
            pl.BlockSpec((d, te), lambda i, s: (0, prev(s))),
            pl.BlockSpec((heads, None, n1_per_step, tm), lambda i, s: (0, 0, prev(s), i)),
            pl.BlockSpec((heads, None, n1_per_step, tm), lambda i, s: (0, 0, cur(s), i)),
            once((heads, None, n_keys, tm), lambda i, s: (0, 1, 0, i)),
            once((heads, n_keys, tm), lambda i, s: (0, 0, i)),
            once((heads, tm), lambda i, s: (0, i)),
            once((heads, tm), lambda i, s: (0, i)),
            once((tm, d), lambda i, s: (i, 0)),
            pl.BlockSpec((1, 1, d), lambda i, s: (i // tps, 0, 0)),
            _resident((1, d), lambda i, s: (0, 0)),
        ],
        out_specs=out_specs,
        out_shape=[jax.ShapeDtypeStruct((r, d), F32) for r in out_rows],
        scratch_shapes=[pltpu.VMEM((d, tm), F32),
                        pltpu.VMEM((te // 2, tm), F32), pltpu.VMEM((te // 2, tm), F32),
                        pltpu.VMEM((te // 2, tm), BF16), pltpu.VMEM((te // 2, tm), BF16),
                        pltpu.VMEM((heads, n1_per_step, tm), F32), pltpu.VMEM((heads, n1_per_step, tm), F32)],
        compiler_params=_cparams("arbitrary", "arbitrary"),
        name="peer_dense_experts",
    )(ht, u, vt, s12, s12, s12, e2, thr, a0, x, gate, fin)


def kernel(x_prompt, x_sample, c_prompt, c_sample, ada_w, ada_b, norm_mix, norm_ffn, ab_in_w, ab_conv_w, ab_conv_b, ab_out_w, cd_in_w, cd_in_b, cd_conv_w, cd_conv_b, cd_ln_g, cd_ln_b, cd_pool_w, cd_pool_scale, cd_out_w, peer_wq, peer_keys, peer_u, peer_v, final_norm):
    bp, seq, d = x_prompt.shape
    batch = bp + x_sample.shape[0]
    assert x_sample.shape[1] == seq
    t = batch * seq
    depth = ada_w.shape[0]
    d_a = ab_conv_w.shape[2]
    d_b = ab_in_w.shape[2] - 3 * d_a
    d_c = cd_conv_w.shape[2]
    d_d = cd_in_w.shape[2] - 2 * d_c
    assert d_a == d_b == d_c == d_d, "column blocks of the combined projections are addressed by block index"
    heads, _, n_keys, dhalf = peer_keys.shape[1:]

    group_rows = [bp * seq, (batch - bp) * seq]
    xs = [x_prompt.reshape(group_rows[0], d), x_sample.reshape(group_rows[1], d)]
    c = jnp.concatenate([c_prompt, c_sample], axis=0)
    c_pad = jnp.pad(c, ((0, -batch % 8), (0, 0)))
    mod = _modulation(c_pad, ada_w, ada_b)[:, :batch].reshape(depth, batch, 6, 1, d)

    dcat, cs = _dft_tables(seq, HEAD_DIM)
    row = lambda v: v.reshape(1, -1)

    for i in range(depth):
        sh_m, sc_m, g_m, sh_f, sc_f, g_f = [mod[i, :, k] for k in range(6)]
        j = i // 2
        if i % 2 == 0:
            w_in = ab_in_w[j].astype(BF16)
            p = _inproj(xs, row(norm_mix[i]), sc_m, sh_m, w_in, jnp.zeros((1, w_in.shape[1]), F32), seq=seq)
            left = _gconv3(p, ab_conv_w[j], row(ab_conv_b[j]), seq=seq, d_a=d_a)
            ab = _dft_channel(p, cs, batch=batch, seq=seq, d_b=d_b, col_block=3)
            right = _dft_seq(dcat, ab.reshape(batch, 2 * seq, d_b)).reshape(t, d_b)
            w_out = ab_out_w[j].astype(BF16)
        else:
            p = _inproj(xs, row(norm_mix[i]), sc_m, sh_m, cd_in_w[j].astype(BF16), row(cd_in_b[j]), seq=seq)
            left = _conf_conv(p, cd_conv_w[j], row(cd_conv_b[j]), row(cd_ln_g[j]), row(cd_ln_b[j]),
                              seq=seq, d_c=d_c)
            right = _pool(p, cd_pool_w[j].astype(BF16), row(cd_pool_scale[j]), seq=seq, d_d=d_d, col_block=2)
            w_out = cd_out_w[j].astype(BF16)
        x = _outproj(left, right, w_out, xs, g_m, seq=seq)

        wk = _fold_keys(peer_keys[i].reshape(heads * 2, n_keys, dhalf), peer_wq[i])
        ht, sct = _peerq(x, row(norm_ffn[i]), sc_f, sh_f, wk, n_keys=n_keys, seq=seq)
        e2, thr, a0 = _stats(sct)
        last = i == depth - 1
        xs = _dense(ht, peer_u[i].astype(BF16), peer_v[i].T.astype(BF16), sct, e2, thr, a0, x, g_f,
                    row(final_norm), seq=seq, final_norm=last, out_rows=group_rows if last else [t])

    return tuple(a.reshape(-1, seq, d) for a in xs)
```

```python
import functools
import math

import jax
import jax.numpy as jnp
from jax import lax
from jax.experimental import pallas as pl
from jax.experimental.pallas import tpu as pltpu

F32 = jnp.float32
BF16 = jnp.bfloat16

LANES = 128
F32_SUBLANES = 8
BF16_SUBLANES = 16
VMEM_LIMIT_BYTES = 56 * 1024 * 1024

HEAD_DIM = 128
SHORT_CONV_W = 3
CONF_CONV_W = 31
POOL_WINDOWS = (2, 4, 8, 16)
PEER_TOPK = 16
NORM_EPS = 1e-6
SQRT_HALF = 0.7071067811865476

HALO = BF16_SUBLANES
assert CONF_CONV_W // 2 < HALO and max(POOL_WINDOWS) // 2 <= HALO

NT_DIMS = (((1,), (1,)), ((), ()))


def _cparams(*semantics, flags=None):
    return pltpu.CompilerParams(dimension_semantics=semantics, vmem_limit_bytes=VMEM_LIMIT_BYTES, flags=flags)


def _resident(block_shape, index_map):
    return pl.BlockSpec(block_shape, index_map, pipeline_mode=pl.Buffered(1))


def _sigmoid(x):
    return 1.0 / (1.0 + jnp.exp(-x))


def _norm_mod(x, g, sc, sh):
    r = lax.rsqrt(jnp.mean(x * x, axis=-1, keepdims=True) + NORM_EPS)
    return (x * r) * g * (1.0 + sc) + sh


def _mod_kernel(c_ref, w_ref, b_ref, o_ref):
    c = c_ref[...]
    a = (c * _sigmoid(c)).astype(BF16)
    o_ref[0] = jnp.dot(a, w_ref[0].astype(BF16), preferred_element_type=F32) + b_ref[0]


def _modulation(c_pad, ada_w, ada_b, *, tn=1024):
    depth, d, n = ada_w.shape
    rows = c_pad.shape[0]
    return pl.pallas_call(
        _mod_kernel,
        grid=(depth, n // tn),
        in_specs=[
            pl.BlockSpec((rows, d), lambda l, j: (0, 0)),
            pl.BlockSpec((1, d, tn), lambda l, j: (l, 0, j)),
            pl.BlockSpec((1, 1, tn), lambda l, j: (l, 0, j)),
        ],
        out_specs=pl.BlockSpec((1, rows, tn), lambda l, j: (l, 0, j)),
        out_shape=jax.ShapeDtypeStruct((depth, rows, n), F32),
        compiler_params=_cparams("parallel", "parallel"),
        name="adaln_modulation",
    )(c_pad, ada_w, ada_b.reshape(depth, 1, n))


def _row_parts(parts, tm):
    specs, starts, lo = [], [], 0
    for p in parts:
        n = p.shape[0] // tm
        specs.append(pl.BlockSpec((tm, p.shape[1]), lambda i, *_, lo=lo, n=n: (jnp.clip(i - lo, 0, n - 1), 0)))
        starts.append(lo)
        lo += n
    return specs, tuple(starts), lo


def _pick_part(refs, starts, cols=slice(None)):
    i = pl.program_id(0)
    x = refs[0][:, cols]
    for ref, lo in zip(refs[1:], starts[1:]):
        x = jnp.where(i >= lo, ref[:, cols], x)
    return x


def _inproj_kernel(*refs, n_chunk, starts):
    x_refs, (g_ref, sc_ref, sh_ref, w_ref, b_ref, o_ref) = refs[:len(starts)], refs[len(starts):]
    h = _norm_mod(_pick_part(x_refs, starts), g_ref[...], sc_ref[0], sh_ref[0]).astype(BF16)
    for n0 in range(0, o_ref.shape[1], n_chunk):
        acc = jnp.dot(h, w_ref[:, n0:n0 + n_chunk], preferred_element_type=F32)
        o_ref[:, n0:n0 + n_chunk] = (acc + b_ref[:, n0:n0 + n_chunk]).astype(o_ref.dtype)


def _inproj(x_parts, g, sc, sh, w, b, *, seq, tm=512, n_chunk=512):
    d = x_parts[0].shape[1]
    n = w.shape[1]
    tps = seq // tm
    x_specs, starts, n_tiles = _row_parts(x_parts, tm)
    t = n_tiles * tm
    return pl.pallas_call(
        functools.partial(_inproj_kernel, n_chunk=n_chunk, starts=starts),
        grid=(n_tiles,),
        in_specs=x_specs + [
            _resident((1, d), lambda i: (0, 0)),
            pl.BlockSpec((1, 1, d), lambda i: (i // tps, 0, 0)),
            pl.BlockSpec((1, 1, d), lambda i: (i // tps, 0, 0)),
            _resident((d, n), lambda i: (0, 0)),
            _resident((1, n), lambda i: (0, 0)),
        ],
        out_specs=pl.BlockSpec((tm, n), lambda i: (i, 0)),
        out_shape=jax.ShapeDtypeStruct((t, n), BF16),
        compiler_params=_cparams("parallel"),
        name="norm_inproj",
    )(*x_parts, g, sc, sh, w, b)


def _outproj_kernel(*refs, n_chunk, starts):
    x_refs, (l_ref, r_ref, w_ref, gate_ref, o_ref) = refs[:len(starts)], refs[len(starts):]
    half = l_ref.shape[1]
    l = l_ref[...]
    r = r_ref[...]
    for n0 in range(0, o_ref.shape[1], n_chunk):
        cols = slice(n0, n0 + n_chunk)
        acc = jnp.dot(l, w_ref[:half, cols], preferred_element_type=F32)
        acc += jnp.dot(r, w_ref[half:, cols], preferred_element_type=F32)
        o_ref[:, cols] = _pick_part(x_refs, starts, cols) + gate_ref[0][:, cols] * acc


def _outproj(l, r, w, x_parts, gate, *, seq, tm=512, n_chunk=512):
    d = x_parts[0].shape[1]
    half = l.shape[1]
    tps = seq // tm
    x_specs, starts, n_tiles = _row_parts(x_parts, tm)
    return pl.pallas_call(
        functools.partial(_outproj_kernel, n_chunk=n_chunk, starts=starts),
        grid=(n_tiles,),
        in_specs=x_specs + [
            pl.BlockSpec((tm, half), lambda i: (i, 0)),
            pl.BlockSpec((tm, half), lambda i: (i, 0)),
            _resident((2 * half, d), lambda i: (0, 0)),
            pl.BlockSpec((1, 1, d), lambda i: (i // tps, 0, 0)),
        ],
        out_specs=pl.BlockSpec((tm, d), lambda i: (i, 0)),
        out_shape=jax.ShapeDtypeStruct((n_tiles * tm, d), F32),
        compiler_params=_cparams("parallel"),
        name="outproj_residual",
    )(*x_parts, l, r, w, gate)


def _gconv3_kernel(gb_ref, gc_ref, v_ref, gcp_ref, vp_ref, gcn_ref, vn_ref, w_ref, b_ref, o_ref, *, tps):
    tm = gb_ref.shape[0]
    si = pl.program_id(0) % tps
    u = gc_ref[...].astype(F32) * v_ref[...].astype(F32)
    u_prev = (gcp_ref[...].astype(F32) * vp_ref[...].astype(F32))[HALO - 1:HALO, :]
    u_next = (gcn_ref[...].astype(F32) * vn_ref[...].astype(F32))[0:1, :]
    u_prev = jnp.where(si == 0, 0.0, u_prev)
    u_next = jnp.where(si == tps - 1, 0.0, u_next)
    row = lax.broadcasted_iota(jnp.int32, (tm, 1), 0)
    u_m1 = jnp.where(row == 0, u_prev, pltpu.roll(u, 1, 0))
    u_p1 = jnp.where(row == tm - 1, u_next, pltpu.roll(u, tm - 1, 0))
    y = w_ref[0:1, :] * u_m1 + w_ref[1:2, :] * u + w_ref[2:3, :] * u_p1 + b_ref[...]
    o_ref[...] = (gb_ref[...].astype(F32) * y).astype(o_ref.dtype)


def _gconv3(p, conv_w, conv_b, *, seq, d_a, tm=256):
    t = p.shape[0]
    tps = seq // tm
    hb = tm // HALO
    last_hb = t // HALO - 1
    main = lambda col: pl.BlockSpec((tm, d_a), lambda i: (i, col))
    prev = lambda col: pl.BlockSpec((HALO, d_a), lambda i: (jnp.maximum(i * hb - 1, 0), col))
    nxt = lambda col: pl.BlockSpec((HALO, d_a), lambda i: (jnp.minimum((i + 1) * hb, last_hb), col))
    return pl.pallas_call(
        functools.partial(_gconv3_kernel, tps=tps),
        grid=(t // tm,),
        in_specs=[main(0), main(1), main(2), prev(1), prev(2), nxt(1), nxt(2),
                  _resident((SHORT_CONV_W, d_a), lambda i: (0, 0)),
                  _resident((1, d_a), lambda i: (0, 0))],
        out_specs=pl.BlockSpec((tm, d_a), lambda i: (i, 0)),
        out_shape=jax.ShapeDtypeStruct((t, d_a), BF16),
        compiler_params=_cparams("parallel"),
        name="gated_conv3",
    )(p, p, p, p, p, p, p, conv_w, conv_b)


def _dft_channel_kernel(f_ref, cs_ref, o_ref):
    for g in range(f_ref.shape[1] // HEAD_DIM):
        c0 = g * HEAD_DIM
        r = jnp.dot(f_ref[:, c0:c0 + HEAD_DIM], cs_ref[...], preferred_element_type=F32)
        o_ref[0, 0, :, c0:c0 + HEAD_DIM] = r[:, :HEAD_DIM].astype(o_ref.dtype)
        o_ref[0, 1, :, c0:c0 + HEAD_DIM] = r[:, HEAD_DIM:].astype(o_ref.dtype)


def _dft_channel(p, cs, *, batch, seq, d_b, col_block, tm=512):
    tps = seq // tm
    return pl.pallas_call(
        _dft_channel_kernel,
        grid=(batch * tps,),
        in_specs=[pl.BlockSpec((tm, d_b), lambda i: (i, col_block)),
                  _resident((HEAD_DIM, 2 * HEAD_DIM), lambda i: (0, 0))],
        out_specs=pl.BlockSpec((1, 2, tm, d_b), lambda i: (i // tps, 0, i % tps, 0)),
        out_shape=jax.ShapeDtypeStruct((batch, 2, seq, d_b), BF16),
        compiler_params=_cparams("parallel"),
        name="dft_channels",
    )(p, cs)


def _dft_seq_kernel(d_ref, ab_ref, o_ref, acc_ref):
    k = pl.program_id(2)

    @pl.when(k == 0)
    def _():
        acc_ref[...] = jnp.zeros_like(acc_ref)

    acc_ref[...] += jnp.dot(d_ref[...], ab_ref[0], preferred_element_type=F32)

    @pl.when(k == pl.num_programs(2) - 1)
    def _():
        o_ref[0] = acc_ref[...].astype(o_ref.dtype)


def _dft_seq(dcat, ab, *, tm=1024, tk=1024):
    batch, two_seq, d_b = ab.shape
    seq = two_seq // 2
    return pl.pallas_call(
        _dft_seq_kernel,
        grid=(batch, seq // tm, two_seq // tk),
        in_specs=[pl.BlockSpec((tm, tk), lambda b, i, k: (i, k)),
                  pl.BlockSpec((1, tk, d_b), lambda b, i, k: (b, k, 0))],
        out_specs=pl.BlockSpec((1, tm, d_b), lambda b, i, k: (b, i, 0)),
        out_shape=jax.ShapeDtypeStruct((batch, seq, d_b), BF16),
        scratch_shapes=[pltpu.VMEM((tm, d_b), F32)],
        compiler_params=_cparams("parallel", "parallel", "arbitrary"),
        name="dft_sequence",
    )(dcat, ab)


def _dft_tables(seq, n):
    lo_n = 1 << (max(seq.bit_length() - 1, 0) // 2)
    assert seq % lo_n == 0
    s = jnp.arange(seq, dtype=jnp.int32)[None, :]
    theta = 2.0 * math.pi / seq
    hi = (jnp.arange(seq // lo_n, dtype=jnp.int32) * lo_n)[:, None]
    lo = jnp.arange(lo_n, dtype=jnp.int32)[:, None]
    ang_hi = ((hi * s) % seq).astype(F32) * theta
    ang_lo = ((lo * s) % seq).astype(F32) * theta
    ch, sh = jnp.cos(ang_hi)[:, None, :], jnp.sin(ang_hi)[:, None, :]
    cl, sl = jnp.cos(ang_lo)[None, :, :], jnp.sin(ang_lo)[None, :, :]
    cos_t = (ch * cl - sh * sl).reshape(seq, seq)
    sin_t = (sh * cl + ch * sl).reshape(seq, seq)
    dcat = (jnp.concatenate([cos_t, -sin_t], axis=1) * (seq ** -0.5)).astype(BF16)
    l = jnp.arange(n, dtype=jnp.int32)
    angn = ((l[:, None] * l[None, :]) % n).astype(F32) * (2.0 * math.pi / n)
    cs = (jnp.concatenate([jnp.cos(angn), jnp.sin(angn)], axis=1) * (n ** -0.5)).astype(BF16)
    return dcat, cs


def _conf_kernel(a_ref, g_ref, ap_ref, gp_ref, an_ref, gn_ref, w_ref, cb_ref, lg_ref, lb_ref, o_ref, upad,
                 ushift, *, tps, row_chunk, copy_chunk):
    tm = a_ref.shape[0]
    si = pl.program_id(0) % tps

    def glu(a, g):
        return a[...].astype(F32) * _sigmoid(g[...].astype(F32))

    upad[0:HALO, :] = jnp.where(si == 0, 0.0, glu(ap_ref, gp_ref))
    upad[HALO:HALO + tm, :] = glu(a_ref, g_ref)
    upad[HALO + tm:, :] = jnp.where(si == tps - 1, 0.0, glu(an_ref, gn_ref))

    half = CONF_CONV_W // 2
    span = tm + (HALO + half) // F32_SUBLANES * F32_SUBLANES
    for r in range(1, F32_SUBLANES):
        for j0 in range(0, span, copy_chunk):
            ushift[r, j0:j0 + copy_chunk, :] = upad[j0 + r:j0 + r + copy_chunk, :]

    for r0 in range(0, tm, row_chunk):
        acc = jnp.zeros((row_chunk, a_ref.shape[1]), F32) + cb_ref[...]
        for k in range(CONF_CONV_W):
            q, r = divmod(HALO + k - half, F32_SUBLANES)
            start = r0 + q * F32_SUBLANES
            rows = upad[start:start + row_chunk, :] if r == 0 else ushift[r, start:start + row_chunk, :]
            acc = acc + w_ref[k:k + 1, :] * rows
        mu = jnp.mean(acc, axis=-1, keepdims=True)
        cen = acc - mu
        var = jnp.mean(cen * cen, axis=-1, keepdims=True)
        y = cen * lax.rsqrt(var + NORM_EPS) * lg_ref[...] + lb_ref[...]
        o_ref[r0:r0 + row_chunk, :] = (y * _sigmoid(y)).astype(o_ref.dtype)


def _conf_conv(p, conv_w, conv_b, ln_g, ln_b, *, seq, d_c, tm=256, row_chunk=16, copy_chunk=40):
    t = p.shape[0]
    tps = seq // tm
    hb = tm // HALO
    last_hb = t // HALO - 1
    main = lambda col: pl.BlockSpec((tm, d_c), lambda i: (i, col))
    prev = lambda col: pl.BlockSpec((HALO, d_c), lambda i: (jnp.maximum(i * hb - 1, 0), col))
    nxt = lambda col: pl.BlockSpec((HALO, d_c), lambda i: (jnp.minimum((i + 1) * hb, last_hb), col))
    vec = lambda rows: _resident((rows, d_c), lambda i: (0, 0))
    return pl.pallas_call(
        functools.partial(_conf_kernel, tps=tps, row_chunk=row_chunk, copy_chunk=copy_chunk),
        grid=(t // tm,),
        in_specs=[main(0), main(1), prev(0), prev(1), nxt(0), nxt(1),
                  vec(CONF_CONV_W), vec(1), vec(1), vec(1)],
        out_specs=pl.BlockSpec((tm, d_c), lambda i: (i, 0)),
        out_shape=jax.ShapeDtypeStruct((t, d_c), BF16),
        scratch_shapes=[pltpu.VMEM((tm + 2 * HALO, d_c), F32),
                        pltpu.VMEM((F32_SUBLANES, tm + 2 * HALO, d_c), F32)],
        compiler_params=_cparams("parallel"),
        name="conformer_conv",
    )(p, p, p, p, p, p, conv_w, conv_b, ln_g, ln_b)


def _pool_kernel(z_ref, zp_ref, zn_ref, pw_ref, ps_ref, o_ref, zpad, *, tps, seq, row_chunk):
    tm = z_ref.shape[0]
    si = pl.program_id(0) % tps
    zpad[0:HALO, :] = jnp.where(si == 0, 0.0, zp_ref[...].astype(F32))
    zpad[HALO:HALO + tm, :] = z_ref[...].astype(F32)
    zpad[HALO + tm:, :] = jnp.where(si == tps - 1, 0.0, zn_ref[...].astype(F32))

    gw = z_ref.shape[1] // len(POOL_WINDOWS)
    for r0 in range(0, tm, row_chunk):
        t_seq = si * tm + r0 + lax.broadcasted_iota(jnp.int32, (row_chunk, 1), 0)
        for gi, w in enumerate(POOL_WINDOWS):
            left = w // 2
            right = w - 1 - left
            c0 = gi * gw
            tot = jnp.zeros((row_chunk, gw), F32)
            for dlt in range(-left, right + 1):
                start = HALO + r0 + dlt
                tot = tot + zpad[start:start + row_chunk, c0:c0 + gw]
            lo = jnp.maximum(t_seq - left, 0)
            hi = jnp.minimum(t_seq + right + 1, seq)
            cnt = (hi - lo).astype(F32)
            pooled = tot / cnt - zpad[HALO + r0:HALO + r0 + row_chunk, c0:c0 + gw]
            yd = jnp.dot(pooled.astype(BF16), pw_ref[gi], preferred_element_type=F32)
            o_ref[r0:r0 + row_chunk, c0:c0 + gw] = (yd * ps_ref[:, c0:c0 + gw]).astype(o_ref.dtype)


def _pool(p, pool_w, pool_scale, *, seq, d_d, col_block, tm=256, row_chunk=64):
    t = p.shape[0]
    tps = seq // tm
    hb = tm // HALO
    last_hb = t // HALO - 1
    n_pool, gw, _ = pool_w.shape
    return pl.pallas_call(
        functools.partial(_pool_kernel, tps=tps, seq=seq, row_chunk=row_chunk),
        grid=(t // tm,),
        in_specs=[pl.BlockSpec((tm, d_d), lambda i: (i, col_block)),
                  pl.BlockSpec((HALO, d_d), lambda i: (jnp.maximum(i * hb - 1, 0), col_block)),
                  pl.BlockSpec((HALO, d_d), lambda i: (jnp.minimum((i + 1) * hb, last_hb), col_block)),
                  _resident((n_pool, gw, gw), lambda i: (0, 0, 0)),
                  _resident((1, d_d), lambda i: (0, 0))],
        out_specs=pl.BlockSpec((tm, d_d), lambda i: (i, 0)),
        out_shape=jax.ShapeDtypeStruct((t, d_d), BF16),
        scratch_shapes=[pltpu.VMEM((tm + 2 * HALO, d_d), F32)],
        compiler_params=_cparams("parallel"),
        name="multiscale_pool",
    )(p, p, p, pool_w, pool_scale)


def _fold_keys_kernel(keys_ref, wq_ref, o_ref):
    o_ref[0] = lax.dot_general(keys_ref[0], wq_ref[...], NT_DIMS, precision=lax.Precision.HIGHEST,
                               preferred_element_type=F32).astype(o_ref.dtype)


def _fold_keys(keys, wq):
    n_hp, n_keys, dhalf = keys.shape
    d = wq.shape[0]
    wk = pl.pallas_call(
        _fold_keys_kernel,
        grid=(n_hp,),
        in_specs=[pl.BlockSpec((1, n_keys, dhalf), lambda j: (j, 0, 0)),
                  pl.BlockSpec((d, dhalf), lambda j: (0, j))],
        out_specs=pl.BlockSpec((1, n_keys, d), lambda j: (j, 0, 0)),
        out_shape=jax.ShapeDtypeStruct((n_hp, n_keys, d), BF16),
        compiler_params=_cparams("parallel"),
        name="peer_fold_keys",
    )(keys, wq)
    return wk.reshape(n_hp * n_keys, d)


def _peerq_kernel(x_ref, g_ref, sc_ref, sh_ref, wk_ref, ht_ref, sct_ref, *, m_chunk):
    hf = _norm_mod(x_ref[...], g_ref[...], sc_ref[0], sh_ref[0])
    ht = hf.T.astype(BF16)
    ht_ref[...] = ht
    n_hp, n_keys, _ = sct_ref.shape
    per_chunk = m_chunk // n_keys
    for hp0 in range(0, n_hp, per_chunk):
        res = jnp.dot(wk_ref[hp0 * n_keys:(hp0 + per_chunk) * n_keys, :], ht, preferred_element_type=F32)
        for c in range(per_chunk):
            sct_ref[hp0 + c] = res[c * n_keys:(c + 1) * n_keys, :]


def _peerq(x, g, sc, sh, wk, *, n_keys, seq, tm=512, m_chunk=512):
    t, d = x.shape
    n_hp = wk.shape[0] // n_keys
    tps = seq // tm
    return pl.pallas_call(
        functools.partial(_peerq_kernel, m_chunk=m_chunk),
        grid=(t // tm,),
        in_specs=[
            pl.BlockSpec((tm, d), lambda i: (i, 0)),
            _resident((1, d), lambda i: (0, 0)),
            pl.BlockSpec((1, 1, d), lambda i: (i // tps, 0, 0)),
            pl.BlockSpec((1, 1, d), lambda i: (i // tps, 0, 0)),
            _resident(wk.shape, lambda i: (0, 0)),
        ],
        out_specs=[pl.BlockSpec((d, tm), lambda i: (0, i)),
                   pl.BlockSpec((n_hp, n_keys, tm), lambda i: (0, 0, i))],
        out_shape=[jax.ShapeDtypeStruct((d, t), BF16),
                   jax.ShapeDtypeStruct((n_hp, n_keys, t), F32)],
        compiler_params=_cparams("parallel"),
        name="peer_queries",
    )(x, g, sc, sh, wk)


def _oddeven_mergesort_pairs(n):
    pairs = []
    p = 1
    while p < n:
        k = p
        while k >= 1:
            for j in range(k % p, n - k, 2 * k):
                for i in range(min(k, n - j - k)):
                    if (i + j) // (2 * p) == (i + j + k) // (2 * p):
                        pairs.append((i + j, i + j + k))
            k //= 2
        p *= 2
    return pairs


def _compare_exchange(x, i, j):
    x[i], x[j] = jnp.maximum(x[i], x[j]), jnp.minimum(x[i], x[j])


def _bitonic_sort_desc(x):
    dist = len(x) // 2
    while dist >= 1:
        for k in range(len(x)):
            if k & dist == 0:
                _compare_exchange(x, k, k + dist)
        dist //= 2


def _merge_top(x, y):
    n = len(x)
    z = [jnp.maximum(x[k], y[n - 1 - k]) if n - 1 - k < len(y) else x[k] for k in range(n)]
    _bitonic_sort_desc(z)
    return z


def _top_rows(s, k):
    x = [s[F32_SUBLANES * g:F32_SUBLANES * (g + 1), :] for g in range(k)]
    for i, j in _oddeven_mergesort_pairs(k):
        _compare_exchange(x, i, j)
    shift = F32_SUBLANES // 2
    while shift >= 1:
        x = _merge_top(x, [pltpu.roll(v, shift, 0) for v in x])
        shift //= 2
    return x


def _stats_kernel(sct_ref, e2_ref, thr_ref, a0_ref):
    heads, n_keys, tl = e2_ref.shape
    k = PEER_TOPK
    assert n_keys == k * F32_SUBLANES and heads == F32_SUBLANES
    sub = lax.broadcasted_iota(jnp.int32, (F32_SUBLANES, tl), 0)
    top1, top2, b0 = None, None, []
    for head in range(heads):
        t1 = _top_rows(sct_ref[2 * head], k)
        t2 = _top_rows(sct_ref[2 * head + 1], k)
        b0.append(t2[0])
        if head == 0:
            top1, top2 = t1, t2
        else:
            top1 = [jnp.where(sub == head, new, old) for new, old in zip(t1, top1)]
            top2 = [jnp.where(sub == head, new, old) for new, old in zip(t2, top2)]

    best = [top1[0] + top2[j] for j in range(k)]
    for i in range(1, k // 2):
        best = _merge_top(best, [top1[i] + top2[j] for j in range(k // (i + 1))])
    best = _merge_top(best, [top1[i] + top2[0] for i in range(k // 2, k)])
    z = functools.reduce(lambda acc, v: acc + jnp.exp(v - best[0]), best[1:], jnp.ones_like(best[0]))
    rz = 1.0 / z
    thr_ref[...] = best[k - 1]
    a0_ref[...] = top1[0]
    for head in range(heads):
        e2_ref[head] = jnp.exp(sct_ref[2 * head + 1] - b0[head][0:1, :]) * rz[head:head + 1, :]


def _stats(sct, *, tl=256):
    n_hp, n_keys, t = sct.shape
    heads = n_hp // 2
    return pl.pallas_call(
        _stats_kernel,
        grid=(t // tl,),
        in_specs=[pl.BlockSpec((n_hp, n_keys, tl), lambda i: (0, 0, i))],
        out_specs=[pl.BlockSpec((heads, n_keys, tl), lambda i: (0, 0, i)),
                   pl.BlockSpec((heads, tl), lambda i: (0, i)),
                   pl.BlockSpec((heads, tl), lambda i: (0, i))],
        out_shape=[jax.ShapeDtypeStruct((heads, n_keys, t), F32),
                   jax.ShapeDtypeStruct((heads, t), F32),
                   jax.ShapeDtypeStruct((heads, t), F32)],
        compiler_params=_cparams("parallel"),
        name="peer_gate_stats",
    )(sct)


def _gated_activation(st_ref, a_ref, s1_ref, e1_ref, row0, t0, s2_ref, e2_ref, thr_ref, *, rows):
    heads, n_keys, _ = s2_ref.shape
    n1 = st_ref.shape[0] // n_keys
    lanes = pl.ds(t0, LANES)
    for r0 in range(0, n_keys, rows):
        w = [jnp.zeros((rows, LANES), F32) for _ in range(n1)]
        for head in range(heads):
            s2 = s2_ref[head, r0:r0 + rows, lanes]
            e2 = e2_ref[head, r0:r0 + rows, lanes]
            thr = thr_ref[head:head + 1, lanes]
            for k in range(n1):
                sel = (s1_ref[head, row0 + k:row0 + k + 1, lanes] + s2) >= thr
                w[k] = w[k] + jnp.where(sel, e2, 0.0) * e1_ref[head, row0 + k:row0 + k + 1, lanes]
        for k in range(n1):
            e0 = k * n_keys + r0
            s = st_ref[e0:e0 + rows, lanes]
            act = 0.5 * s * (1.0 + lax.erf(s * SQRT_HALF))
            a_ref[e0:e0 + rows, lanes] = (act * w[k]).astype(a_ref.dtype)


def _dense_kernel(ht_ref, u_ref, vt_ref, s1p_ref, s1c_ref, s2_ref, e2_ref, thr_ref, a0_ref, x_ref, gate_ref,
                  fin_ref, *refs, rows, n_slices, final_norm, out_starts):
    o_refs = refs[:len(out_starts)]
    acc_ref, st0_ref, st1_ref, a0s_ref, a1s_ref, e1p_ref, e1c_ref = refs[len(out_starts):]
    s = pl.program_id(1)
    heads, _, tm = s2_ref.shape
    half = st0_ref.shape[0]
    d = acc_ref.shape[0]
    n1_half = s1c_ref.shape[1] // 2
    t_slice = tm // n_slices
    u_rows = half // n_slices
    v_rows = d // n_slices

    first = s == 0
    last = s == pl.num_programs(1) - 1

    @pl.when(first)
    def _():
        acc_ref[...] = jnp.zeros_like(acc_ref)

    for head in range(heads):
        e1p_ref[head] = jnp.exp(s1p_ref[head] - a0_ref[head:head + 1, :])
        e1c_ref[head] = jnp.exp(s1c_ref[head] - a0_ref[head:head + 1, :])

    gates = functools.partial(_gated_activation, s2_ref=s2_ref, e2_ref=e2_ref, thr_ref=thr_ref, rows=rows)

    def half_block(u0, v0, st_out, st_in, a_out, a_in, s1_ref, e1_ref, row0, *, score, gate, mix):
        def body(p, carry):
            if score:
                ur = pl.ds(pl.multiple_of(u0 + p * u_rows, u_rows), u_rows)
                st_out[pl.ds(pl.multiple_of(p * u_rows, u_rows), u_rows), :] = jnp.dot(
                    u_ref[ur, :], ht_ref[...], preferred_element_type=F32)
            if mix:
                vr = pl.ds(pl.multiple_of(p * v_rows, v_rows), v_rows)
                acc_ref[vr, :] += jnp.dot(vt_ref[vr, v0:v0 + half], a_in[...], preferred_element_type=F32)
            if gate:
                for c in range(0, t_slice, LANES):
                    gates(st_in, a_out, s1_ref, e1_ref, row0, pl.multiple_of(p * t_slice + c, LANES))
            return carry
        lax.fori_loop(0, n_slices, body, 0)

    first_half = functools.partial(half_block, 0, 0, st0_ref, st1_ref, a1s_ref, a0s_ref, s1p_ref, e1p_ref, n1_half)
    second_half = functools.partial(half_block, half, half, st1_ref, st0_ref, a0s_ref, a1s_ref, s1c_ref, e1c_ref, 0)

    @pl.when(first)
    def _():
        first_half(score=True, gate=False, mix=False)
        second_half(score=True, gate=True, mix=False)

    @pl.when(jnp.logical_not(first | last))
    def _():
        first_half(score=True, gate=True, mix=True)
        second_half(score=True, gate=True, mix=True)

    @pl.when(last)
    def _():
        first_half(score=False, gate=True, mix=True)
        second_half(score=False, gate=False, mix=True)

    def finish(o_ref):
        for d0 in range(0, d, tm):
            o_ref[:, d0:d0 + tm] = x_ref[:, d0:d0 + tm] + gate_ref[0][:, d0:d0 + tm] * acc_ref[d0:d0 + tm, :].T
        if final_norm:
            y = o_ref[...]
            r = lax.rsqrt(jnp.mean(y * y, axis=-1, keepdims=True) + NORM_EPS)
            o_ref[...] = (y * r) * fin_ref[...]

    i = pl.program_id(0)
    last = s == pl.num_programs(1) - 1
    bounds = out_starts[1:] + (pl.num_programs(0),)
    for o_ref, lo, hi in zip(o_refs, out_starts, bounds):
        pl.when(last & (i >= lo) & (i < hi))(functools.partial(finish, o_ref))


def _dense(ht, u, vt, sct, e2, thr, a0, x, gate, fin, *, seq, final_norm, out_rows, tm=512, rows=32,
           n_slices=2):
    t, d = x.shape
    assert sum(out_rows) == t and all(r % tm == 0 for r in out_rows)
    out_specs, out_starts, _ = _row_parts([jax.ShapeDtypeStruct((r, d), F32) for r in out_rows], tm)
    n_exp = u.shape[0]
    n_hp, n_keys, _ = sct.shape
    heads = n_hp // 2
    n1_per_step = 8
    te = n1_per_step * n_keys
    nj = n_exp // te
    tps = seq // tm
    s12 = sct.reshape(heads, 2, n_keys, t)
    once = functools.partial(pl.BlockSpec, pipeline_mode=pl.Buffered(1))
    prev = lambda s: jnp.maximum(s - 1, 0)
    cur = lambda s: jnp.minimum(s, nj - 1)
    return pl.pallas_call(
        functools.partial(_dense_kernel, rows=rows, n_slices=n_slices, final_norm=final_norm,
                          out_starts=out_starts),
        grid=(t // tm, nj + 1),
        in_specs=[
            once((d, tm), lambda i, s: (0, i)),
            pl.BlockSpec((te, d), lambda i, s: (cur(s), 0)),
            pl.BlockSpec((d, te), lambda i, s: (0, prev(s))),
            pl.BlockSpec((heads, None, n1_per_step, tm), lambda i, s: (0, 0, prev(s), i)),
            pl.BlockSpec((heads, None, n1_per_step, tm), lambda i, s: (0, 0, cur(s), i)),
            once((heads, None, n_keys, tm), lambda i, s: (0, 1, 0, i)),
            once((heads, n_keys, tm), lambda i, s: (0, 0, i)),
            once((heads, tm), lambda i, s: (0, i)),
            once((heads, tm), lambda i, s: (0, i)),
            once((tm, d), lambda i, s: (i, 0)),
            pl.BlockSpec((1, 1, d), lambda i, s: (i // tps, 0, 0)),
            _resident((1, d), lambda i, s: (0, 0)),
        ],
        out_specs=out_specs,
        out_shape=[jax.ShapeDtypeStruct((r, d), F32) for r in out_rows],
        scratch_shapes=[pltpu.VMEM((d, tm), F32),
                        pltpu.VMEM((te // 2, tm), F32), pltpu.VMEM((te // 2, tm), F32),
                        pltpu.VMEM((te // 2, tm), BF16), pltpu.VMEM((te // 2, tm), BF16),
                        pltpu.VMEM((heads, n1_per_step, tm), F32), pltpu.VMEM((heads, n1_per_step, tm), F32)],
        compiler_params=_cparams("arbitrary", "arbitrary"),
        name="peer_dense_experts",
    )(ht, u, vt, s12, s12, s12, e2, thr, a0, x, gate, fin)


def kernel(x_prompt, x_sample, c_prompt, c_sample, ada_w, ada_b, norm_mix, norm_ffn, ab_in_w, ab_conv_w, ab_conv_b, ab_out_w, cd_in_w, cd_in_b, cd_conv_w, cd_conv_b, cd_ln_g, cd_ln_b, cd_pool_w, cd_pool_scale, cd_out_w, peer_wq, peer_keys, peer_u, peer_v, final_norm):
    bp, seq, d = x_prompt.shape
    batch = bp + x_sample.shape[0]
    assert x_sample.shape[1] == seq
    t = batch * seq
    depth = ada_w.shape[0]
    d_a = ab_conv_w.shape[2]
    d_b = ab_in_w.shape[2] - 3 * d_a
    d_c = cd_conv_w.shape[2]
    d_d = cd_in_w.shape[2] - 2 * d_c
    assert d_a == d_b == d_c == d_d, "column blocks of the combined projections are addressed by block index"
    heads, _, n_keys, dhalf = peer_keys.shape[1:]

    group_rows = [bp * seq, (batch - bp) * seq]
    xs = [x_prompt.reshape(group_rows[0], d), x_sample.reshape(group_rows[1], d)]
    c = jnp.concatenate([c_prompt, c_sample], axis=0)
    c_pad = jnp.pad(c, ((0, -batch % 8), (0, 0)))
    mod = _modulation(c_pad, ada_w, ada_b)[:, :batch].reshape(depth, batch, 6, 1, d)

    dcat, cs = _dft_tables(seq, HEAD_DIM)
    row = lambda v: v.reshape(1, -1)

    for i in range(depth):
        sh_m, sc_m, g_m, sh_f, sc_f, g_f = [mod[i, :, k] for k in range(6)]
        j = i // 2
        if i % 2 == 0:
            w_in = ab_in_w[j].astype(BF16)
            p = _inproj(xs, row(norm_mix[i]), sc_m, sh_m, w_in, jnp.zeros((1, w_in.shape[1]), F32), seq=seq)
            left = _gconv3(p, ab_conv_w[j], row(ab_conv_b[j]), seq=seq, d_a=d_a)
            ab = _dft_channel(p, cs, batch=batch, seq=seq, d_b=d_b, col_block=3)
            right = _dft_seq(dcat, ab.reshape(batch, 2 * seq, d_b)).reshape(t, d_b)
            w_out = ab_out_w[j].astype(BF16)
        else:
            p = _inproj(xs, row(norm_mix[i]), sc_m, sh_m, cd_in_w[j].astype(BF16), row(cd_in_b[j]), seq=seq)
            left = _conf_conv(p, cd_conv_w[j], row(cd_conv_b[j]), row(cd_ln_g[j]), row(cd_ln_b[j]),
                              seq=seq, d_c=d_c)
            right = _pool(p, cd_pool_w[j].astype(BF16), row(cd_pool_scale[j]), seq=seq, d_d=d_d, col_block=2)
            w_out = cd_out_w[j].astype(BF16)
        x = _outproj(left, right, w_out, xs, g_m, seq=seq)

        wk = _fold_keys(peer_keys[i].reshape(heads * 2, n_keys, dhalf), peer_wq[i])
        ht, sct = _peerq(x, row(norm_ffn[i]), sc_f, sh_f, wk, n_keys=n_keys, seq=seq)
        e2, thr, a0 = _stats(sct)
        last = i == depth - 1
        xs = _dense(ht, peer_u[i].astype(BF16), peer_v[i].T.astype(BF16), sct, e2, thr, a0, x, g_f,
                    row(final_norm), seq=seq, final_norm=last, out_rows=group_rows if last else [t])

    return tuple(a.reshape(-1, seq, d) for a in xs)
```

```python
import functools
import math

import jax
import jax.numpy as jnp
from jax import lax
from jax.experimental import pallas as pl
from jax.experimental.pallas import tpu as pltpu

F32 = jnp.float32
BF16 = jnp.bfloat16

LANES = 128
F32_SUBLANES = 8
BF16_SUBLANES = 16
VMEM_LIMIT_BYTES = 56 * 1024 * 1024

HEAD_DIM = 128
SHORT_CONV_W = 3
CONF_CONV_W = 31
POOL_WINDOWS = (2, 4, 8, 16)
PEER_TOPK = 16
NORM_EPS = 1e-6
SQRT_HALF = 0.7071067811865476

HALO = BF16_SUBLANES
assert CONF_CONV_W // 2 < HALO and max(POOL_WINDOWS) // 2 <= HALO

NT_DIMS = (((1,), (1,)), ((), ()))


def _cparams(*semantics, flags=None):
    return pltpu.CompilerParams(dimension_semantics=semantics, vmem_limit_bytes=VMEM_LIMIT_BYTES, flags=flags)


def _resident(block_shape, index_map):
    return pl.BlockSpec(block_shape, index_map, pipeline_mode=pl.Buffered(1))


def _sigmoid(x):
    return 1.0 / (1.0 + jnp.exp(-x))


def _norm_mod(x, g, sc, sh):
    r = lax.rsqrt(jnp.mean(x * x, axis=-1, keepdims=True) + NORM_EPS)
    return (x * r) * g * (1.0 + sc) + sh


def _mod_kernel(c_ref, w_ref, b_ref, o_ref):
    c = c_ref[...]
    a = (c * _sigmoid(c)).astype(BF16)
    o_ref[0] = jnp.dot(a, w_ref[0].astype(BF16), preferred_element_type=F32) + b_ref[0]


def _modulation(c_pad, ada_w, ada_b, *, tn=1024):
    depth, d, n = ada_w.shape
    rows = c_pad.shape[0]
    return pl.pallas_call(
        _mod_kernel,
        grid=(depth, n // tn),
        in_specs=[
            pl.BlockSpec((rows, d), lambda l, j: (0, 0)),
            pl.BlockSpec((1, d, tn), lambda l, j: (l, 0, j)),
            pl.BlockSpec((1, 1, tn), lambda l, j: (l, 0, j)),
        ],
        out_specs=pl.BlockSpec((1, rows, tn), lambda l, j: (l, 0, j)),
        out_shape=jax.ShapeDtypeStruct((depth, rows, n), F32),
        compiler_params=_cparams("parallel", "parallel"),
        name="adaln_modulation",
    )(c_pad, ada_w, ada_b.reshape(depth, 1, n))


def _row_parts(parts, tm):
    specs, starts, lo = [], [], 0
    for p in parts:
        n = p.shape[0] // tm
        specs.append(pl.BlockSpec((tm, p.shape[1]), lambda i, *_, lo=lo, n=n: (jnp.clip(i - lo, 0, n - 1), 0)))
        starts.append(lo)
        lo += n
    return specs, tuple(starts), lo


def _pick_part(refs, starts, cols=slice(None)):
    i = pl.program_id(0)
    x = refs[0][:, cols]
    for ref, lo in zip(refs[1:], starts[1:]):
        x = jnp.where(i >= lo, ref[:, cols], x)
    return x


def _inproj_kernel(*refs, n_chunk, starts):
    x_refs, (g_ref, sc_ref, sh_ref, w_ref, b_ref, o_ref) = refs[:len(starts)], refs[len(starts):]
    h = _norm_mod(_pick_part(x_refs, starts), g_ref[...], sc_ref[0], sh_ref[0]).astype(BF16)
    for n0 in range(0, o_ref.shape[1], n_chunk):
        acc = jnp.dot(h, w_ref[:, n0:n0 + n_chunk], preferred_element_type=F32)
        o_ref[:, n0:n0 + n_chunk] = (acc + b_ref[:, n0:n0 + n_chunk]).astype(o_ref.dtype)


def _inproj(x_parts, g, sc, sh, w, b, *, seq, tm=512, n_chunk=512):
    d = x_parts[0].shape[1]
    n = w.shape[1]
    tps = seq // tm
    x_specs, starts, n_tiles = _row_parts(x_parts, tm)
    t = n_tiles * tm
    return pl.pallas_call(
        functools.partial(_inproj_kernel, n_chunk=n_chunk, starts=starts),
        grid=(n_tiles,),
        in_specs=x_specs + [
            _resident((1, d), lambda i: (0, 0)),
            pl.BlockSpec((1, 1, d), lambda i: (i // tps, 0, 0)),
            pl.BlockSpec((1, 1, d), lambda i: (i // tps, 0, 0)),
            _resident((d, n), lambda i: (0, 0)),
            _resident((1, n), lambda i: (0, 0)),
        ],
        out_specs=pl.BlockSpec((tm, n), lambda i: (i, 0)),
        out_shape=jax.ShapeDtypeStruct((t, n), BF16),
        compiler_params=_cparams("parallel"),
        name="norm_inproj",
    )(*x_parts, g, sc, sh, w, b)


def _outproj_kernel(*refs, n_chunk, starts):
    x_refs, (l_ref, r_ref, w_ref, gate_ref, o_ref) = refs[:len(starts)], refs[len(starts):]
    half = l_ref.shape[1]
    l = l_ref[...]
    r = r_ref[...]
    for n0 in range(0, o_ref.shape[1], n_chunk):
        cols = slice(n0, n0 + n_chunk)
        acc = jnp.dot(l, w_ref[:half, cols], preferred_element_type=F32)
        acc += jnp.dot(r, w_ref[half:, cols], preferred_element_type=F32)
        o_ref[:, cols] = _pick_part(x_refs, starts, cols) + gate_ref[0][:, cols] * acc


def _outproj(l, r, w, x_parts, gate, *, seq, tm=512, n_chunk=512):
    d = x_parts[0].shape[1]
    half = l.shape[1]
    tps = seq // tm
    x_specs, starts, n_tiles = _row_parts(x_parts, tm)
    return pl.pallas_call(
        functools.partial(_outproj_kernel, n_chunk=n_chunk, starts=starts),
        grid=(n_tiles,),
        in_specs=x_specs + [
            pl.BlockSpec((tm, half), lambda i: (i, 0)),
            pl.BlockSpec((tm, half), lambda i: (i, 0)),
            _resident((2 * half, d), lambda i: (0, 0)),
            pl.BlockSpec((1, 1, d), lambda i: (i // tps, 0, 0)),
        ],
        out_specs=pl.BlockSpec((tm, d), lambda i: (i, 0)),
        out_shape=jax.ShapeDtypeStruct((n_tiles * tm, d), F32),
        compiler_params=_cparams("parallel"),
        name="outproj_residual",
    )(*x_parts, l, r, w, gate)


def _gconv3_kernel(gb_ref, gc_ref, v_ref, gcp_ref, vp_ref, gcn_ref, vn_ref, w_ref, b_ref, o_ref, *, tps):
    tm = gb_ref.shape[0]
    si = pl.program_id(0) % tps
    u = gc_ref[...].astype(F32) * v_ref[...].astype(F32)
    u_prev = (gcp_ref[...].astype(F32) * vp_ref[...].astype(F32))[HALO - 1:HALO, :]
    u_next = (gcn_ref[...].astype(F32) * vn_ref[...].astype(F32))[0:1, :]
    u_prev = jnp.where(si == 0, 0.0, u_prev)
    u_next = jnp.where(si == tps - 1, 0.0, u_next)
    row = lax.broadcasted_iota(jnp.int32, (tm, 1), 0)
    u_m1 = jnp.where(row == 0, u_prev, pltpu.roll(u, 1, 0))
    u_p1 = jnp.where(row == tm - 1, u_next, pltpu.roll(u, tm - 1, 0))
    y = w_ref[0:1, :] * u_m1 + w_ref[1:2, :] * u + w_ref[2:3, :] * u_p1 + b_ref[...]
    o_ref[...] = (gb_ref[...].astype(F32) * y).astype(o_ref.dtype)


def _gconv3(p, conv_w, conv_b, *, seq, d_a, tm=256):
    t = p.shape[0]
    tps = seq // tm
    hb = tm // HALO
    last_hb = t // HALO - 1
    main = lambda col: pl.BlockSpec((tm, d_a), lambda i: (i, col))
    prev = lambda col: pl.BlockSpec((HALO, d_a), lambda i: (jnp.maximum(i * hb - 1, 0), col))
    nxt = lambda col: pl.BlockSpec((HALO, d_a), lambda i: (jnp.minimum((i + 1) * hb, last_hb), col))
    return pl.pallas_call(
        functools.partial(_gconv3_kernel, tps=tps),
        grid=(t // tm,),
        in_specs=[main(0), main(1), main(2), prev(1), prev(2), nxt(1), nxt(2),
                  _resident((SHORT_CONV_W, d_a), lambda i: (0, 0)),
                  _resident((1, d_a), lambda i: (0, 0))],
        out_specs=pl.BlockSpec((tm, d_a), lambda i: (i, 0)),
        out_shape=jax.ShapeDtypeStruct((t, d_a), BF16),
        compiler_params=_cparams("parallel"),
        name="gated_conv3",
    )(p, p, p, p, p, p, p, conv_w, conv_b)


def _dft_channel_kernel(f_ref, cs_ref, o_ref):
    for g in range(f_ref.shape[1] // HEAD_DIM):
        c0 = g * HEAD_DIM
        r = jnp.dot(f_ref[:, c0:c0 + HEAD_DIM], cs_ref[...], preferred_element_type=F32)
        o_ref[0, 0, :, c0:c0 + HEAD_DIM] = r[:, :HEAD_DIM].astype(o_ref.dtype)
        o_ref[0, 1, :, c0:c0 + HEAD_DIM] = r[:, HEAD_DIM:].astype(o_ref.dtype)


def _dft_channel(p, cs, *, batch, seq, d_b, col_block, tm=512):
    tps = seq // tm
    return pl.pallas_call(
        _dft_channel_kernel,
        grid=(batch * tps,),
        in_specs=[pl.BlockSpec((tm, d_b), lambda i: (i, col_block)),
                  _resident((HEAD_DIM, 2 * HEAD_DIM), lambda i: (0, 0))],
        out_specs=pl.BlockSpec((1, 2, tm, d_b), lambda i: (i // tps, 0, i % tps, 0)),
        out_shape=jax.ShapeDtypeStruct((batch, 2, seq, d_b), BF16),
        compiler_params=_cparams("parallel"),
        name="dft_channels",
    )(p, cs)


def _dft_seq_kernel(d_ref, ab_ref, o_ref, acc_ref):
    k = pl.program_id(2)

    @pl.when(k == 0)
    def _():
        acc_ref[...] = jnp.zeros_like(acc_ref)

    acc_ref[...] += jnp.dot(d_ref[...], ab_ref[0], preferred_element_type=F32)

    @pl.when(k == pl.num_programs(2) - 1)
    def _():
        o_ref[0] = acc_ref[...].astype(o_ref.dtype)


def _dft_seq(dcat, ab, *, tm=1024, tk=1024):
    batch, two_seq, d_b = ab.shape
    seq = two_seq // 2
    return pl.pallas_call(
        _dft_seq_kernel,
        grid=(batch, seq // tm, two_seq // tk),
        in_specs=[pl.BlockSpec((tm, tk), lambda b, i, k: (i, k)),
                  pl.BlockSpec((1, tk, d_b), lambda b, i, k: (b, k, 0))],
        out_specs=pl.BlockSpec((1, tm, d_b), lambda b, i, k: (b, i, 0)),
        out_shape=jax.ShapeDtypeStruct((batch, seq, d_b), BF16),
        scratch_shapes=[pltpu.VMEM((tm, d_b), F32)],
        compiler_params=_cparams("parallel", "parallel", "arbitrary"),
        name="dft_sequence",
    )(dcat, ab)


def _dft_tables(seq, n):
    lo_n = 1 << (max(seq.bit_length() - 1, 0) // 2)
    assert seq % lo_n == 0
    s = jnp.arange(seq, dtype=jnp.int32)[None, :]
    theta = 2.0 * math.pi / seq
    hi = (jnp.arange(seq // lo_n, dtype=jnp.int32) * lo_n)[:, None]
    lo = jnp.arange(lo_n, dtype=jnp.int32)[:, None]
    ang_hi = ((hi * s) % seq).astype(F32) * theta
    ang_lo = ((lo * s) % seq).astype(F32) * theta
    ch, sh = jnp.cos(ang_hi)[:, None, :], jnp.sin(ang_hi)[:, None, :]
    cl, sl = jnp.cos(ang_lo)[None, :, :], jnp.sin(ang_lo)[None, :, :]
    cos_t = (ch * cl - sh * sl).reshape(seq, seq)
    sin_t = (sh * cl + ch * sl).reshape(seq, seq)
    dcat = (jnp.concatenate([cos_t, -sin_t], axis=1) * (seq ** -0.5)).astype(BF16)
    l = jnp.arange(n, dtype=jnp.int32)
    angn = ((l[:, None] * l[None, :]) % n).astype(F32) * (2.0 * math.pi / n)
    cs = (jnp.concatenate([jnp.cos(angn), jnp.sin(angn)], axis=1) * (n ** -0.5)).astype(BF16)
    return dcat, cs


def _conf_kernel(a_ref, g_ref, ap_ref, gp_ref, an_ref, gn_ref, w_ref, cb_ref, lg_ref, lb_ref, o_ref, upad,
                 ushift, *, tps, row_chunk, copy_chunk):
    tm = a_ref.shape[0]
    si = pl.program_id(0) % tps

    def glu(a, g):
        return a[...].astype(F32) * _sigmoid(g[...].astype(F32))

    upad[0:HALO, :] = jnp.where(si == 0, 0.0, glu(ap_ref, gp_ref))
    upad[HALO:HALO + tm, :] = glu(a_ref, g_ref)
    upad[HALO + tm:, :] = jnp.where(si == tps - 1, 0.0, glu(an_ref, gn_ref))

    half = CONF_CONV_W // 2
    span = tm + (HALO + half) // F32_SUBLANES * F32_SUBLANES
    for r in range(1, F32_SUBLANES):
        for j0 in range(0, span, copy_chunk):
            ushift[r, j0:j0 + copy_chunk, :] = upad[j0 + r:j0 + r + copy_chunk, :]

    for r0 in range(0, tm, row_chunk):
        acc = jnp.zeros((row_chunk, a_ref.shape[1]), F32) + cb_ref[...]
        for k in range(CONF_CONV_W):
            q, r = divmod(HALO + k - half, F32_SUBLANES)
            start = r0 + q * F32_SUBLANES
            rows = upad[start:start + row_chunk, :] if r == 0 else ushift[r, start:start + row_chunk, :]
            acc = acc + w_ref[k:k + 1, :] * rows
        mu = jnp.mean(acc, axis=-1, keepdims=True)
        cen = acc - mu
        var = jnp.mean(cen * cen, axis=-1, keepdims=True)
        y = cen * lax.rsqrt(var + NORM_EPS) * lg_ref[...] + lb_ref[...]
        o_ref[r0:r0 + row_chunk, :] = (y * _sigmoid(y)).astype(o_ref.dtype)


def _conf_conv(p, conv_w, conv_b, ln_g, ln_b, *, seq, d_c, tm=256, row_chunk=16, copy_chunk=40):
    t = p.shape[0]
    tps = seq // tm
    hb = tm // HALO
    last_hb = t // HALO - 1
    main = lambda col: pl.BlockSpec((tm, d_c), lambda i: (i, col))
    prev = lambda col: pl.BlockSpec((HALO, d_c), lambda i: (jnp.maximum(i * hb - 1, 0), col))
    nxt = lambda col: pl.BlockSpec((HALO, d_c), lambda i: (jnp.minimum((i + 1) * hb, last_hb), col))
    vec = lambda rows: _resident((rows, d_c), lambda i: (0, 0))
    return pl.pallas_call(
        functools.partial(_conf_kernel, tps=tps, row_chunk=row_chunk, copy_chunk=copy_chunk),
        grid=(t // tm,),
        in_specs=[main(0), main(1), prev(0), prev(1), nxt(0), nxt(1),
                  vec(CONF_CONV_W), vec(1), vec(1), vec(1)],
        out_specs=pl.BlockSpec((tm, d_c), lambda i: (i, 0)),
        out_shape=jax.ShapeDtypeStruct((t, d_c), BF16),
        scratch_shapes=[pltpu.VMEM((tm + 2 * HALO, d_c), F32),
                        pltpu.VMEM((F32_SUBLANES, tm + 2 * HALO, d_c), F32)],
        compiler_params=_cparams("parallel"),
        name="conformer_conv",
    )(p, p, p, p, p, p, conv_w, conv_b, ln_g, ln_b)


def _pool_kernel(z_ref, zp_ref, zn_ref, pw_ref, ps_ref, o_ref, zpad, *, tps, seq, row_chunk):
    tm = z_ref.shape[0]
    si = pl.program_id(0) % tps
    zpad[0:HALO, :] = jnp.where(si == 0, 0.0, zp_ref[...].astype(F32))
    zpad[HALO:HALO + tm, :] = z_ref[...].astype(F32)
    zpad[HALO + tm:, :] = jnp.where(si == tps - 1, 0.0, zn_ref[...].astype(F32))

    gw = z_ref.shape[1] // len(POOL_WINDOWS)
    for r0 in range(0, tm, row_chunk):
        t_seq = si * tm + r0 + lax.broadcasted_iota(jnp.int32, (row_chunk, 1), 0)
        for gi, w in enumerate(POOL_WINDOWS):
            left = w // 2
            right = w - 1 - left
            c0 = gi * gw
            tot = jnp.zeros((row_chunk, gw), F32)
            for dlt in range(-left, right + 1):
                start = HALO + r0 + dlt
                tot = tot + zpad[start:start + row_chunk, c0:c0 + gw]
            lo = jnp.maximum(t_seq - left, 0)
            hi = jnp.minimum(t_seq + right + 1, seq)
            cnt = (hi - lo).astype(F32)
            pooled = tot / cnt - zpad[HALO + r0:HALO + r0 + row_chunk, c0:c0 + gw]
            yd = jnp.dot(pooled.astype(BF16), pw_ref[gi], preferred_element_type=F32)
            o_ref[r0:r0 + row_chunk, c0:c0 + gw] = (yd * ps_ref[:, c0:c0 + gw]).astype(o_ref.dtype)


def _pool(p, pool_w, pool_scale, *, seq, d_d, col_block, tm=256, row_chunk=64):
    t = p.shape[0]
    tps = seq // tm
    hb = tm // HALO
    last_hb = t // HALO - 1
    n_pool, gw, _ = pool_w.shape
    return pl.pallas_call(
        functools.partial(_pool_kernel, tps=tps, seq=seq, row_chunk=row_chunk),
        grid=(t // tm,),
        in_specs=[pl.BlockSpec((tm, d_d), lambda i: (i, col_block)),
                  pl.BlockSpec((HALO, d_d), lambda i: (jnp.maximum(i * hb - 1, 0), col_block)),
                  pl.BlockSpec((HALO, d_d), lambda i: (jnp.minimum((i + 1) * hb, last_hb), col_block)),
                  _resident((n_pool, gw, gw), lambda i: (0, 0, 0)),
                  _resident((1, d_d), lambda i: (0, 0))],
        out_specs=pl.BlockSpec((tm, d_d), lambda i: (i, 0)),
        out_shape=jax.ShapeDtypeStruct((t, d_d), BF16),
        scratch_shapes=[pltpu.VMEM((tm + 2 * HALO, d_d), F32)],
        compiler_params=_cparams("parallel"),
        name="multiscale_pool",
    )(p, p, p, pool_w, pool_scale)


def _fold_keys_kernel(keys_ref, wq_ref, o_ref):
    o_ref[0] = lax.dot_general(keys_ref[0], wq_ref[...], NT_DIMS, precision=lax.Precision.HIGHEST,
                               preferred_element_type=F32).astype(o_ref.dtype)


def _fold_keys(keys, wq):
    n_hp, n_keys, dhalf = keys.shape
    d = wq.shape[0]
    wk = pl.pallas_call(
        _fold_keys_kernel,
        grid=(n_hp,),
        in_specs=[pl.BlockSpec((1, n_keys, dhalf), lambda j: (j, 0, 0)),
                  pl.BlockSpec((d, dhalf), lambda j: (0, j))],
        out_specs=pl.BlockSpec((1, n_keys, d), lambda j: (j, 0, 0)),
        out_shape=jax.ShapeDtypeStruct((n_hp, n_keys, d), BF16),
        compiler_params=_cparams("parallel"),
        name="peer_fold_keys",
    )(keys, wq)
    return wk.reshape(n_hp * n_keys, d)


def _peerq_kernel(x_ref, g_ref, sc_ref, sh_ref, wk_ref, ht_ref, sct_ref, *, m_chunk):
    hf = _norm_mod(x_ref[...], g_ref[...], sc_ref[0], sh_ref[0])
    ht = hf.T.astype(BF16)
    ht_ref[...] = ht
    n_hp, n_keys, _ = sct_ref.shape
    per_chunk = m_chunk // n_keys
    for hp0 in range(0, n_hp, per_chunk):
        res = jnp.dot(wk_ref[hp0 * n_keys:(hp0 + per_chunk) * n_keys, :], ht, preferred_element_type=F32)
        for c in range(per_chunk):
            sct_ref[hp0 + c] = res[c * n_keys:(c + 1) * n_keys, :]


def _peerq(x, g, sc, sh, wk, *, n_keys, seq, tm=512, m_chunk=512):
    t, d = x.shape
    n_hp = wk.shape[0] // n_keys
    tps = seq // tm
    return pl.pallas_call(
        functools.partial(_peerq_kernel, m_chunk=m_chunk),
        grid=(t // tm,),
        in_specs=[
            pl.BlockSpec((tm, d), lambda i: (i, 0)),
            _resident((1, d), lambda i: (0, 0)),
            pl.BlockSpec((1, 1, d), lambda i: (i // tps, 0, 0)),
            pl.BlockSpec((1, 1, d), lambda i: (i // tps, 0, 0)),
            _resident(wk.shape, lambda i: (0, 0)),
        ],
        out_specs=[pl.BlockSpec((d, tm), lambda i: (0, i)),
                   pl.BlockSpec((n_hp, n_keys, tm), lambda i: (0, 0, i))],
        out_shape=[jax.ShapeDtypeStruct((d, t), BF16),
                   jax.ShapeDtypeStruct((n_hp, n_keys, t), F32)],
        compiler_params=_cparams("parallel"),
        name="peer_queries",
    )(x, g, sc, sh, wk)


def _oddeven_mergesort_pairs(n):
    pairs = []
    p = 1
    while p < n:
        k = p
        while k >= 1:
            for j in range(k % p, n - k, 2 * k):
                for i in range(min(k, n - j - k)):
                    if (i + j) // (2 * p) == (i + j + k) // (2 * p):
                        pairs.append((i + j, i + j + k))
            k //= 2
        p *= 2
    return pairs


def _compare_exchange(x, i, j):
    x[i], x[j] = jnp.maximum(x[i], x[j]), jnp.minimum(x[i], x[j])


def _bitonic_sort_desc(x):
    dist = len(x) // 2
    while dist >= 1:
        for k in range(len(x)):
            if k & dist == 0:
                _compare_exchange(x, k, k + dist)
        dist //= 2


def _merge_top(x, y):
    n = len(x)
    z = [jnp.maximum(x[k], y[n - 1 - k]) if n - 1 - k < len(y) else x[k] for k in range(n)]
    _bitonic_sort_desc(z)
    return z


def _top_rows(s, k):
    x = [s[F32_SUBLANES * g:F32_SUBLANES * (g + 1), :] for g in range(k)]
    for i, j in _oddeven_mergesort_pairs(k):
        _compare_exchange(x, i, j)
    shift = F32_SUBLANES // 2
    while shift >= 1:
        x = _merge_top(x, [pltpu.roll(v, shift, 0) for v in x])
        shift //= 2
    return x


def _stats_kernel(sct_ref, e2_ref, thr_ref, a0_ref):
    heads, n_keys, tl = e2_ref.shape
    k = PEER_TOPK
    assert n_keys == k * F32_SUBLANES and heads == F32_SUBLANES
    sub = lax.broadcasted_iota(jnp.int32, (F32_SUBLANES, tl), 0)
    top1, top2, b0 = None, None, []
    for head in range(heads):
        t1 = _top_rows(sct_ref[2 * head], k)
        t2 = _top_rows(sct_ref[2 * head + 1], k)
        b0.append(t2[0])
        if head == 0:
            top1, top2 = t1, t2
        else:
            top1 = [jnp.where(sub == head, new, old) for new, old in zip(t1, top1)]
            top2 = [jnp.where(sub == head, new, old) for new, old in zip(t2, top2)]

    best = [top1[0] + top2[j] for j in range(k)]
    for i in range(1, k // 2):
        best = _merge_top(best, [top1[i] + top2[j] for j in range(k // (i + 1))])
    best = _merge_top(best, [top1[i] + top2[0] for i in range(k // 2, k)])
    z = functools.reduce(lambda acc, v: acc + jnp.exp(v - best[0]), best[1:], jnp.ones_like(best[0]))
    rz = 1.0 / z
    thr_ref[...] = best[k - 1]
    a0_ref[...] = top1[0]
    for head in range(heads):
        e2_ref[head] = jnp.exp(sct_ref[2 * head + 1] - b0[head][0:1, :]) * rz[head:head + 1, :]


def _stats(sct, *, tl=256):
    n_hp, n_keys, t = sct.shape
    heads = n_hp // 2
    return pl.pallas_call(
        _stats_kernel,
        grid=(t // tl,),
        in_specs=[pl.BlockSpec((n_hp, n_keys, tl), lambda i: (0, 0, i))],
        out_specs=[pl.BlockSpec((heads, n_keys, tl), lambda i: (0, 0, i)),
                   pl.BlockSpec((heads, tl), lambda i: (0, i)),
                   pl.BlockSpec((heads, tl), lambda i: (0, i))],
        out_shape=[jax.ShapeDtypeStruct((heads, n_keys, t), F32),
                   jax.ShapeDtypeStruct((heads, t), F32),
                   jax.ShapeDtypeStruct((heads, t), F32)],
        compiler_params=_cparams("parallel"),
        name="peer_gate_stats",
    )(sct)


def _gated_activation(st_ref, a_ref, s1_ref, e1_ref, row0, t0, s2_ref, e2_ref, thr_ref, *, rows):
    heads, n_keys, _ = s2_ref.shape
    n1 = st_ref.shape[0] // n_keys
    lanes = pl.ds(t0, LANES)
    for r0 in range(0, n_keys, rows):
        w = [jnp.zeros((rows, LANES), F32) for _ in range(n1)]
        for head in range(heads):
            s2 = s2_ref[head, r0:r0 + rows, lanes]
            e2 = e2_ref[head, r0:r0 + rows, lanes]
            thr = thr_ref[head:head + 1, lanes]
            for k in range(n1):
                sel = (s1_ref[head, row0 + k:row0 + k + 1, lanes] + s2) >= thr
                w[k] = w[k] + jnp.where(sel, e2, 0.0) * e1_ref[head, row0 + k:row0 + k + 1, lanes]
        for k in range(n1):
            e0 = k * n_keys + r0
            s = st_ref[e0:e0 + rows, lanes]
            act = 0.5 * s * (1.0 + lax.erf(s * SQRT_HALF))
            a_ref[e0:e0 + rows, lanes] = (act * w[k]).astype(a_ref.dtype)


def _dense_kernel(ht_ref, u_ref, vt_ref, s1p_ref, s1c_ref, s2_ref, e2_ref, thr_ref, a0_ref, x_ref, gate_ref,
                  fin_ref, *refs, rows, n_slices, final_norm, out_starts):
    o_refs = refs[:len(out_starts)]
    acc_ref, st0_ref, st1_ref, a0s_ref, a1s_ref, e1p_ref, e1c_ref = refs[len(out_starts):]
    s = pl.program_id(1)
    heads, _, tm = s2_ref.shape
    half = st0_ref.shape[0]
    d = acc_ref.shape[0]
    n1_half = s1c_ref.shape[1] // 2
    t_slice = tm // n_slices
    u_rows = half // n_slices
    v_rows = d // n_slices

    first = s == 0
    last = s == pl.num_programs(1) - 1

    @pl.when(first)
    def _():
        acc_ref[...] = jnp.zeros_like(acc_ref)

    for head in range(heads):
        e1p_ref[head] = jnp.exp(s1p_ref[head] - a0_ref[head:head + 1, :])
        e1c_ref[head] = jnp.exp(s1c_ref[head] - a0_ref[head:head + 1, :])

    gates = functools.partial(_gated_activation, s2_ref=s2_ref, e2_ref=e2_ref, thr_ref=thr_ref, rows=rows)

    def half_block(u0, v0, st_out, st_in, a_out, a_in, s1_ref, e1_ref, row0, *, score, gate, mix):
        def body(p, carry):
            if score:
                ur = pl.ds(pl.multiple_of(u0 + p * u_rows, u_rows), u_rows)
                st_out[pl.ds(pl.multiple_of(p * u_rows, u_rows), u_rows), :] = jnp.dot(
                    u_ref[ur, :], ht_ref[...], preferred_element_type=F32)
            if mix:
                vr = pl.ds(pl.multiple_of(p * v_rows, v_rows), v_rows)
                acc_ref[vr, :] += jnp.dot(vt_ref[vr, v0:v0 + half], a_in[...], preferred_element_type=F32)
            if gate:
                for c in range(0, t_slice, LANES):
                    gates(st_in, a_out, s1_ref, e1_ref, row0, pl.multiple_of(p * t_slice + c, LANES))
            return carry
        lax.fori_loop(0, n_slices, body, 0)

    first_half = functools.partial(half_block, 0, 0, st0_ref, st1_ref, a1s_ref, a0s_ref, s1p_ref, e1p_ref, n1_half)
    second_half = functools.partial(half_block, half, half, st1_ref, st0_ref, a0s_ref, a1s_ref, s1c_ref, e1c_ref, 0)

    @pl.when(first)
    def _():
        first_half(score=True, gate=False, mix=False)
        second_half(score=True, gate=True, mix=False)

    @pl.when(jnp.logical_not(first | last))
    def _():
        first_half(score=True, gate=True, mix=True)
        second_half(score=True, gate=True, mix=True)

    @pl.when(last)
    def _():
        first_half(score=False, gate=True, mix=True)
        second_half(score=False, gate=False, mix=True)

    def finish(o_ref):
        for d0 in range(0, d, tm):
            o_ref[:, d0:d0 + tm] = x_ref[:, d0:d0 + tm] + gate_ref[0][:, d0:d0 + tm] * acc_ref[d0:d0 + tm, :].T
        if final_norm:
            y = o_ref[...]
            r = lax.rsqrt(jnp.mean(y * y, axis=-1, keepdims=True) + NORM_EPS)
            o_ref[...] = (y * r) * fin_ref[...]

    i = pl.program_id(0)
    last = s == pl.num_programs(1) - 1
    bounds = out_starts[1:] + (pl.num_programs(0),)
    for o_ref, lo, hi in zip(o_refs, out_starts, bounds):
        pl.when(last & (i >= lo) & (i < hi))(functools.partial(finish, o_ref))


def _dense(ht, u, vt, sct, e2, thr, a0, x, gate, fin, *, seq, final_norm, out_rows, tm=512, rows=32,
           n_slices=2):
    t, d = x.shape
    assert sum(out_rows) == t and all(r % tm == 0 for r in out_rows)
    out_specs, out_starts, _ = _row_parts([jax.ShapeDtypeStruct((r, d), F32) for r in out_rows], tm)
    n_exp = u.shape[0]
    n_hp, n_keys, _ = sct.shape
    heads = n_hp // 2
    n1_per_step = 8
    te = n1_per_step * n_keys
    nj = n_exp // te
    tps = seq // tm
    s12 = sct.reshape(heads, 2, n_keys, t)
    once = functools.partial(pl.BlockSpec, pipeline_mode=pl.Buffered(1))
    prev = lambda s: jnp.maximum(s - 1, 0)
    cur = lambda s: jnp.minimum(s, nj - 1)
    return pl.pallas_call(
        functools.partial(_dense_kernel, rows=rows, n_slices=n_slices, final_norm=final_norm,
                          out_starts=out_starts),
        grid=(t // tm, nj + 1),
        in_specs=[
            once((d, tm), lambda i, s: (0, i)),
            pl.BlockSpec((te, d), lambda i, s: (cur(s), 0)),
            pl.BlockSpec((d, te), lambda i, s: (0, prev(s))),
            pl.BlockSpec((heads, None, n1_per_step, tm), lambda i, s: (0, 0, prev(s), i)),
            pl.BlockSpec((heads, None, n1_per_step, tm), lambda i, s: (0, 0, cur(s), i)),
            once((heads, None, n_keys, tm), lambda i, s: (0, 1, 0, i)),
            once((heads, n_keys, tm), lambda i, s: (0, 0, i)),
            once((heads, tm), lambda i, s: (0, i)),
            once((heads, tm), lambda i, s: (0, i)),
            once((tm, d), lambda i, s: (i, 0)),
            pl.BlockSpec((1, 1, d), lambda i, s: (i // tps, 0, 0)),
            _resident((1, d), lambda i, s: (0, 0)),
        ],
        out_specs=out_specs,
        out_shape=[jax.ShapeDtypeStruct((r, d), F32) for r in out_rows],
        scratch_shapes=[pltpu.VMEM((d, tm), F32),
                        pltpu.VMEM((te // 2, tm), F32), pltpu.VMEM((te // 2, tm), F32),
                        pltpu.VMEM((te // 2, tm), BF16), pltpu.VMEM((te // 2, tm), BF16),
                        pltpu.VMEM((heads, n1_per_step, tm), F32), pltpu.VMEM((heads, n1_per_step, tm), F32)],
        compiler_params=_cparams("arbitrary", "arbitrary"),
        name="peer_dense_experts",
    )(ht, u, vt, s12, s12, s12, e2, thr, a0, x, gate, fin)


def _transpose_cast_kernel(v_ref, o_ref):
    o_ref[...] = v_ref[...].T.astype(o_ref.dtype)


def _transpose_cast(tables, layer, *, tr=512):
    _, e, d = tables.shape
    return pl.pallas_call(
        _transpose_cast_kernel,
        grid=(e // tr,),
        in_specs=[pl.BlockSpec((None, tr, d), lambda j: (layer, j, 0))],
        out_specs=pl.BlockSpec((d, tr), lambda j: (0, j)),
        out_shape=jax.ShapeDtypeStruct((d, e), BF16),
        compiler_params=_cparams("parallel"),
        name="expert_values_transposed",
    )(tables)


def kernel(x_prompt, x_sample, c_prompt, c_sample, ada_w, ada_b, norm_mix, norm_ffn, ab_in_w, ab_conv_w, ab_conv_b, ab_out_w, cd_in_w, cd_in_b, cd_conv_w, cd_conv_b, cd_ln_g, cd_ln_b, cd_pool_w, cd_pool_scale, cd_out_w, peer_wq, peer_keys, peer_u, peer_v, final_norm):
    bp, seq, d = x_prompt.shape
    batch = bp + x_sample.shape[0]
    assert x_sample.shape[1] == seq
    t = batch * seq
    depth = ada_w.shape[0]
    d_a = ab_conv_w.shape[2]
    d_b = ab_in_w.shape[2] - 3 * d_a
    d_c = cd_conv_w.shape[2]
    d_d = cd_in_w.shape[2] - 2 * d_c
    assert d_a == d_b == d_c == d_d, "column blocks of the combined projections are addressed by block index"
    heads, _, n_keys, dhalf = peer_keys.shape[1:]

    group_rows = [bp * seq, (batch - bp) * seq]
    xs = [x_prompt.reshape(group_rows[0], d), x_sample.reshape(group_rows[1], d)]
    c = jnp.concatenate([c_prompt, c_sample], axis=0)
    c_pad = jnp.pad(c, ((0, -batch % 8), (0, 0)))
    mod = _modulation(c_pad, ada_w, ada_b)[:, :batch].reshape(depth, batch, 6, 1, d)

    dcat, cs = _dft_tables(seq, HEAD_DIM)
    row = lambda v: v.reshape(1, -1)

    for i in range(depth):
        sh_m, sc_m, g_m, sh_f, sc_f, g_f = [mod[i, :, k] for k in range(6)]
        j = i // 2
        if i % 2 == 0:
            w_in = ab_in_w[j].astype(BF16)
            p = _inproj(xs, row(norm_mix[i]), sc_m, sh_m, w_in, jnp.zeros((1, w_in.shape[1]), F32), seq=seq)
            left = _gconv3(p, ab_conv_w[j], row(ab_conv_b[j]), seq=seq, d_a=d_a)
            ab = _dft_channel(p, cs, batch=batch, seq=seq, d_b=d_b, col_block=3)
            right = _dft_seq(dcat, ab.reshape(batch, 2 * seq, d_b)).reshape(t, d_b)
            w_out = ab_out_w[j].astype(BF16)
        else:
            p = _inproj(xs, row(norm_mix[i]), sc_m, sh_m, cd_in_w[j].astype(BF16), row(cd_in_b[j]), seq=seq)
            left = _conf_conv(p, cd_conv_w[j], row(cd_conv_b[j]), row(cd_ln_g[j]), row(cd_ln_b[j]),
                              seq=seq, d_c=d_c)
            right = _pool(p, cd_pool_w[j].astype(BF16), row(cd_pool_scale[j]), seq=seq, d_d=d_d, col_block=2)
            w_out = cd_out_w[j].astype(BF16)
        x = _outproj(left, right, w_out, xs, g_m, seq=seq)

        wk = _fold_keys(peer_keys[i].reshape(heads * 2, n_keys, dhalf), peer_wq[i])
        ht, sct = _peerq(x, row(norm_ffn[i]), sc_f, sh_f, wk, n_keys=n_keys, seq=seq)
        e2, thr, a0 = _stats(sct)
        last = i == depth - 1
        xs = _dense(ht, peer_u[i].astype(BF16), _transpose_cast(peer_v, i), sct, e2, thr, a0, x, g_f,
                    row(final_norm), seq=seq, final_norm=last, out_rows=group_rows if last else [t])

    return tuple(a.reshape(-1, seq, d) for a in xs)
```

```python
import functools
import math

import jax
import jax.numpy as jnp
from jax import lax
from jax.experimental import pallas as pl
from jax.experimental.pallas import tpu as pltpu

F32 = jnp.float32
BF16 = jnp.bfloat16

LANES = 128
F32_SUBLANES = 8
BF16_SUBLANES = 16
VMEM_LIMIT_BYTES = 56 * 1024 * 1024

HEAD_DIM = 128
SHORT_CONV_W = 3
CONF_CONV_W = 31
POOL_WINDOWS = (2, 4, 8, 16)
PEER_TOPK = 16
NORM_EPS = 1e-6
SQRT_HALF = 0.7071067811865476

HALO = BF16_SUBLANES
assert CONF_CONV_W // 2 < HALO and max(POOL_WINDOWS) // 2 <= HALO

NT_DIMS = (((1,), (1,)), ((), ()))


def _cparams(*semantics, flags=None):
    return pltpu.CompilerParams(dimension_semantics=semantics, vmem_limit_bytes=VMEM_LIMIT_BYTES, flags=flags)


def _resident(block_shape, index_map):
    return pl.BlockSpec(block_shape, index_map, pipeline_mode=pl.Buffered(1))


def _sigmoid(x):
    return 1.0 / (1.0 + jnp.exp(-x))


def _norm_mod(x, g, sc, sh):
    r = lax.rsqrt(jnp.mean(x * x, axis=-1, keepdims=True) + NORM_EPS)
    return (x * r) * g * (1.0 + sc) + sh


def _mod_kernel(c_ref, w_ref, b_ref, o_ref):
    c = c_ref[...]
    a = (c * _sigmoid(c)).astype(BF16)
    o_ref[0] = jnp.dot(a, w_ref[0].astype(BF16), preferred_element_type=F32) + b_ref[0]


def _modulation(c_pad, ada_w, ada_b, *, tn=1024):
    depth, d, n = ada_w.shape
    rows = c_pad.shape[0]
    return pl.pallas_call(
        _mod_kernel,
        grid=(depth, n // tn),
        in_specs=[
            pl.BlockSpec((rows, d), lambda l, j: (0, 0)),
            pl.BlockSpec((1, d, tn), lambda l, j: (l, 0, j)),
            pl.BlockSpec((1, 1, tn), lambda l, j: (l, 0, j)),
        ],
        out_specs=pl.BlockSpec((1, rows, tn), lambda l, j: (l, 0, j)),
        out_shape=jax.ShapeDtypeStruct((depth, rows, n), F32),
        compiler_params=_cparams("parallel", "parallel"),
        name="adaln_modulation",
    )(c_pad, ada_w, ada_b.reshape(depth, 1, n))


def _row_parts(parts, tm):
    specs, starts, lo = [], [], 0
    for p in parts:
        n = p.shape[0] // tm
        specs.append(pl.BlockSpec((tm, p.shape[1]), lambda i, *_, lo=lo, n=n: (jnp.clip(i - lo, 0, n - 1), 0)))
        starts.append(lo)
        lo += n
    return specs, tuple(starts), lo


def _pick_part(refs, starts, cols=slice(None)):
    i = pl.program_id(0)
    x = refs[0][:, cols]
    for ref, lo in zip(refs[1:], starts[1:]):
        x = jnp.where(i >= lo, ref[:, cols], x)
    return x


def _inproj_kernel(*refs, n_chunk, starts):
    x_refs, (g_ref, sc_ref, sh_ref, w_ref, b_ref, o_ref) = refs[:len(starts)], refs[len(starts):]
    h = _norm_mod(_pick_part(x_refs, starts), g_ref[...], sc_ref[0], sh_ref[0]).astype(BF16)
    for n0 in range(0, o_ref.shape[1], n_chunk):
        acc = jnp.dot(h, w_ref[:, n0:n0 + n_chunk], preferred_element_type=F32)
        o_ref[:, n0:n0 + n_chunk] = (acc + b_ref[:, n0:n0 + n_chunk]).astype(o_ref.dtype)


def _inproj(x_parts, g, sc, sh, w, b, *, seq, tm=512, n_chunk=512):
    d = x_parts[0].shape[1]
    n = w.shape[1]
    tps = seq // tm
    x_specs, starts, n_tiles = _row_parts(x_parts, tm)
    t = n_tiles * tm
    return pl.pallas_call(
        functools.partial(_inproj_kernel, n_chunk=n_chunk, starts=starts),
        grid=(n_tiles,),
        in_specs=x_specs + [
            _resident((1, d), lambda i: (0, 0)),
            pl.BlockSpec((1, 1, d), lambda i: (i // tps, 0, 0)),
            pl.BlockSpec((1, 1, d), lambda i: (i // tps, 0, 0)),
            _resident((d, n), lambda i: (0, 0)),
            _resident((1, n), lambda i: (0, 0)),
        ],
        out_specs=pl.BlockSpec((tm, n), lambda i: (i, 0)),
        out_shape=jax.ShapeDtypeStruct((t, n), BF16),
        compiler_params=_cparams("parallel"),
        name="norm_inproj",
    )(*x_parts, g, sc, sh, w, b)


def _outproj_kernel(*refs, n_chunk, starts):
    x_refs, (l_ref, r_ref, w_ref, gate_ref, o_ref) = refs[:len(starts)], refs[len(starts):]
    half = l_ref.shape[1]
    l = l_ref[...]
    r = r_ref[...]
    for n0 in range(0, o_ref.shape[1], n_chunk):
        cols = slice(n0, n0 + n_chunk)
        acc = jnp.dot(l, w_ref[:half, cols], preferred_element_type=F32)
        acc += jnp.dot(r, w_ref[half:, cols], preferred_element_type=F32)
        o_ref[:, cols] = _pick_part(x_refs, starts, cols) + gate_ref[0][:, cols] * acc


def _outproj(l, r, w, x_parts, gate, *, seq, tm=512, n_chunk=512):
    d = x_parts[0].shape[1]
    half = l.shape[1]
    tps = seq // tm
    x_specs, starts, n_tiles = _row_parts(x_parts, tm)
    return pl.pallas_call(
        functools.partial(_outproj_kernel, n_chunk=n_chunk, starts=starts),
        grid=(n_tiles,),
        in_specs=x_specs + [
            pl.BlockSpec((tm, half), lambda i: (i, 0)),
            pl.BlockSpec((tm, half), lambda i: (i, 0)),
            _resident((2 * half, d), lambda i: (0, 0)),
            pl.BlockSpec((1, 1, d), lambda i: (i // tps, 0, 0)),
        ],
        out_specs=pl.BlockSpec((tm, d), lambda i: (i, 0)),
        out_shape=jax.ShapeDtypeStruct((n_tiles * tm, d), F32),
        compiler_params=_cparams("parallel"),
        name="outproj_residual",
    )(*x_parts, l, r, w, gate)


def _gconv3_kernel(gb_ref, gc_ref, v_ref, gcp_ref, vp_ref, gcn_ref, vn_ref, w_ref, b_ref, o_ref, *, tps):
    tm = gb_ref.shape[0]
    si = pl.program_id(0) % tps
    u = gc_ref[...].astype(F32) * v_ref[...].astype(F32)
    u_prev = (gcp_ref[...].astype(F32) * vp_ref[...].astype(F32))[HALO - 1:HALO, :]
    u_next = (gcn_ref[...].astype(F32) * vn_ref[...].astype(F32))[0:1, :]
    u_prev = jnp.where(si == 0, 0.0, u_prev)
    u_next = jnp.where(si == tps - 1, 0.0, u_next)
    row = lax.broadcasted_iota(jnp.int32, (tm, 1), 0)
    u_m1 = jnp.where(row == 0, u_prev, pltpu.roll(u, 1, 0))
    u_p1 = jnp.where(row == tm - 1, u_next, pltpu.roll(u, tm - 1, 0))
    y = w_ref[0:1, :] * u_m1 + w_ref[1:2, :] * u + w_ref[2:3, :] * u_p1 + b_ref[...]
    o_ref[...] = (gb_ref[...].astype(F32) * y).astype(o_ref.dtype)


def _gconv3(p, conv_w, conv_b, *, seq, d_a, tm=256):
    t = p.shape[0]
    tps = seq // tm
    hb = tm // HALO
    last_hb = t // HALO - 1
    main = lambda col: pl.BlockSpec((tm, d_a), lambda i: (i, col))
    prev = lambda col: pl.BlockSpec((HALO, d_a), lambda i: (jnp.maximum(i * hb - 1, 0), col))
    nxt = lambda col: pl.BlockSpec((HALO, d_a), lambda i: (jnp.minimum((i + 1) * hb, last_hb), col))
    return pl.pallas_call(
        functools.partial(_gconv3_kernel, tps=tps),
        grid=(t // tm,),
        in_specs=[main(0), main(1), main(2), prev(1), prev(2), nxt(1), nxt(2),
                  _resident((SHORT_CONV_W, d_a), lambda i: (0, 0)),
                  _resident((1, d_a), lambda i: (0, 0))],
        out_specs=pl.BlockSpec((tm, d_a), lambda i: (i, 0)),
        out_shape=jax.ShapeDtypeStruct((t, d_a), BF16),
        compiler_params=_cparams("parallel"),
        name="gated_conv3",
    )(p, p, p, p, p, p, p, conv_w, conv_b)


def _dft_channel_kernel(f_ref, cs_ref, o_ref):
    for g in range(f_ref.shape[1] // HEAD_DIM):
        c0 = g * HEAD_DIM
        r = jnp.dot(f_ref[:, c0:c0 + HEAD_DIM], cs_ref[...], preferred_element_type=F32)
        o_ref[0, 0, :, c0:c0 + HEAD_DIM] = r[:, :HEAD_DIM].astype(o_ref.dtype)
        o_ref[0, 1, :, c0:c0 + HEAD_DIM] = r[:, HEAD_DIM:].astype(o_ref.dtype)


def _dft_channel(p, cs, *, batch, seq, d_b, col_block, tm=512):
    tps = seq // tm
    return pl.pallas_call(
        _dft_channel_kernel,
        grid=(batch * tps,),
        in_specs=[pl.BlockSpec((tm, d_b), lambda i: (i, col_block)),
                  _resident((HEAD_DIM, 2 * HEAD_DIM), lambda i: (0, 0))],
        out_specs=pl.BlockSpec((1, 2, tm, d_b), lambda i: (i // tps, 0, i % tps, 0)),
        out_shape=jax.ShapeDtypeStruct((batch, 2, seq, d_b), BF16),
        compiler_params=_cparams("parallel"),
        name="dft_channels",
    )(p, cs)


def _dft_seq_kernel(d_ref, ab_ref, o_ref, acc_ref):
    k = pl.program_id(2)

    @pl.when(k == 0)
    def _():
        acc_ref[...] = jnp.zeros_like(acc_ref)

    acc_ref[...] += jnp.dot(d_ref[...], ab_ref[0], preferred_element_type=F32)

    @pl.when(k == pl.num_programs(2) - 1)
    def _():
        o_ref[0] = acc_ref[...].astype(o_ref.dtype)


def _dft_seq(dcat, ab, *, tm=1024, tk=1024):
    batch, two_seq, d_b = ab.shape
    seq = two_seq // 2
    return pl.pallas_call(
        _dft_seq_kernel,
        grid=(batch, seq // tm, two_seq // tk),
        in_specs=[pl.BlockSpec((tm, tk), lambda b, i, k: (i, k)),
                  pl.BlockSpec((1, tk, d_b), lambda b, i, k: (b, k, 0))],
        out_specs=pl.BlockSpec((1, tm, d_b), lambda b, i, k: (b, i, 0)),
        out_shape=jax.ShapeDtypeStruct((batch, seq, d_b), BF16),
        scratch_shapes=[pltpu.VMEM((tm, d_b), F32)],
        compiler_params=_cparams("parallel", "parallel", "arbitrary"),
        name="dft_sequence",
    )(dcat, ab)


def _dft_tables(seq, n):
    lo_n = 1 << (max(seq.bit_length() - 1, 0) // 2)
    assert seq % lo_n == 0
    s = jnp.arange(seq, dtype=jnp.int32)[None, :]
    theta = 2.0 * math.pi / seq
    hi = (jnp.arange(seq // lo_n, dtype=jnp.int32) * lo_n)[:, None]
    lo = jnp.arange(lo_n, dtype=jnp.int32)[:, None]
    ang_hi = ((hi * s) % seq).astype(F32) * theta
    ang_lo = ((lo * s) % seq).astype(F32) * theta
    ch, sh = jnp.cos(ang_hi)[:, None, :], jnp.sin(ang_hi)[:, None, :]
    cl, sl = jnp.cos(ang_lo)[None, :, :], jnp.sin(ang_lo)[None, :, :]
    cos_t = (ch * cl - sh * sl).reshape(seq, seq)
    sin_t = (sh * cl + ch * sl).reshape(seq, seq)
    dcat = (jnp.concatenate([cos_t, -sin_t], axis=1) * (seq ** -0.5)).astype(BF16)
    l = jnp.arange(n, dtype=jnp.int32)
    angn = ((l[:, None] * l[None, :]) % n).astype(F32) * (2.0 * math.pi / n)
    cs = (jnp.concatenate([jnp.cos(angn), jnp.sin(angn)], axis=1) * (n ** -0.5)).astype(BF16)
    return dcat, cs


def _conf_kernel(a_ref, g_ref, ap_ref, gp_ref, an_ref, gn_ref, w_ref, cb_ref, lg_ref, lb_ref, o_ref, upad,
                 ushift, *, tps, row_chunk, copy_chunk):
    tm = a_ref.shape[0]
    si = pl.program_id(0) % tps

    def glu(a, g):
        return a[...].astype(F32) * _sigmoid(g[...].astype(F32))

    upad[0:HALO, :] = jnp.where(si == 0, 0.0, glu(ap_ref, gp_ref))
    upad[HALO:HALO + tm, :] = glu(a_ref, g_ref)
    upad[HALO + tm:, :] = jnp.where(si == tps - 1, 0.0, glu(an_ref, gn_ref))

    half = CONF_CONV_W // 2
    span = tm + (HALO + half) // F32_SUBLANES * F32_SUBLANES
    for r in range(1, F32_SUBLANES):
        for j0 in range(0, span, copy_chunk):
            ushift[r, j0:j0 + copy_chunk, :] = upad[j0 + r:j0 + r + copy_chunk, :]

    for r0 in range(0, tm, row_chunk):
        acc = jnp.zeros((row_chunk, a_ref.shape[1]), F32) + cb_ref[...]
        for k in range(CONF_CONV_W):
            q, r = divmod(HALO + k - half, F32_SUBLANES)
            start = r0 + q * F32_SUBLANES
            rows = upad[start:start + row_chunk, :] if r == 0 else ushift[r, start:start + row_chunk, :]
            acc = acc + w_ref[k:k + 1, :] * rows
        mu = jnp.mean(acc, axis=-1, keepdims=True)
        cen = acc - mu
        var = jnp.mean(cen * cen, axis=-1, keepdims=True)
        y = cen * lax.rsqrt(var + NORM_EPS) * lg_ref[...] + lb_ref[...]
        o_ref[r0:r0 + row_chunk, :] = (y * _sigmoid(y)).astype(o_ref.dtype)


def _conf_conv(p, conv_w, conv_b, ln_g, ln_b, *, seq, d_c, tm=256, row_chunk=16, copy_chunk=40):
    t = p.shape[0]
    tps = seq // tm
    hb = tm // HALO
    last_hb = t // HALO - 1
    main = lambda col: pl.BlockSpec((tm, d_c), lambda i: (i, col))
    prev = lambda col: pl.BlockSpec((HALO, d_c), lambda i: (jnp.maximum(i * hb - 1, 0), col))
    nxt = lambda col: pl.BlockSpec((HALO, d_c), lambda i: (jnp.minimum((i + 1) * hb, last_hb), col))
    vec = lambda rows: _resident((rows, d_c), lambda i: (0, 0))
    return pl.pallas_call(
        functools.partial(_conf_kernel, tps=tps, row_chunk=row_chunk, copy_chunk=copy_chunk),
        grid=(t // tm,),
        in_specs=[main(0), main(1), prev(0), prev(1), nxt(0), nxt(1),
                  vec(CONF_CONV_W), vec(1), vec(1), vec(1)],
        out_specs=pl.BlockSpec((tm, d_c), lambda i: (i, 0)),
        out_shape=jax.ShapeDtypeStruct((t, d_c), BF16),
        scratch_shapes=[pltpu.VMEM((tm + 2 * HALO, d_c), F32),
                        pltpu.VMEM((F32_SUBLANES, tm + 2 * HALO, d_c), F32)],
        compiler_params=_cparams("parallel"),
        name="conformer_conv",
    )(p, p, p, p, p, p, conv_w, conv_b, ln_g, ln_b)


def _pool_kernel(z_ref, zp_ref, zn_ref, pw_ref, ps_ref, o_ref, zpad, *, tps, seq, row_chunk):
    tm = z_ref.shape[0]
    si = pl.program_id(0) % tps
    zpad[0:HALO, :] = jnp.where(si == 0, 0.0, zp_ref[...].astype(F32))
    zpad[HALO:HALO + tm, :] = z_ref[...].astype(F32)
    zpad[HALO + tm:, :] = jnp.where(si == tps - 1, 0.0, zn_ref[...].astype(F32))

    gw = z_ref.shape[1] // len(POOL_WINDOWS)
    for r0 in range(0, tm, row_chunk):
        t_seq = si * tm + r0 + lax.broadcasted_iota(jnp.int32, (row_chunk, 1), 0)
        for gi, w in enumerate(POOL_WINDOWS):
            left = w // 2
            right = w - 1 - left
            c0 = gi * gw
            tot = jnp.zeros((row_chunk, gw), F32)
            for dlt in range(-left, right + 1):
                start = HALO + r0 + dlt
                tot = tot + zpad[start:start + row_chunk, c0:c0 + gw]
            lo = jnp.maximum(t_seq - left, 0)
            hi = jnp.minimum(t_seq + right + 1, seq)
            cnt = (hi - lo).astype(F32)
            pooled = tot / cnt - zpad[HALO + r0:HALO + r0 + row_chunk, c0:c0 + gw]
            yd = jnp.dot(pooled.astype(BF16), pw_ref[gi], preferred_element_type=F32)
            o_ref[r0:r0 + row_chunk, c0:c0 + gw] = (yd * ps_ref[:, c0:c0 + gw]).astype(o_ref.dtype)


def _pool(p, pool_w, pool_scale, *, seq, d_d, col_block, tm=256, row_chunk=64):
    t = p.shape[0]
    tps = seq // tm
    hb = tm // HALO
    last_hb = t // HALO - 1
    n_pool, gw, _ = pool_w.shape
    return pl.pallas_call(
        functools.partial(_pool_kernel, tps=tps, seq=seq, row_chunk=row_chunk),
        grid=(t // tm,),
        in_specs=[pl.BlockSpec((tm, d_d), lambda i: (i, col_block)),
                  pl.BlockSpec((HALO, d_d), lambda i: (jnp.maximum(i * hb - 1, 0), col_block)),
                  pl.BlockSpec((HALO, d_d), lambda i: (jnp.minimum((i + 1) * hb, last_hb), col_block)),
                  _resident((n_pool, gw, gw), lambda i: (0, 0, 0)),
                  _resident((1, d_d), lambda i: (0, 0))],
        out_specs=pl.BlockSpec((tm, d_d), lambda i: (i, 0)),
        out_shape=jax.ShapeDtypeStruct((t, d_d), BF16),
        scratch_shapes=[pltpu.VMEM((tm + 2 * HALO, d_d), F32)],
        compiler_params=_cparams("parallel"),
        name="multiscale_pool",
    )(p, p, p, pool_w, pool_scale)


def _fold_keys_kernel(keys_ref, wq_ref, o_ref):
    o_ref[0] = lax.dot_general(keys_ref[0], wq_ref[...], NT_DIMS, precision=lax.Precision.HIGHEST,
                               preferred_element_type=F32).astype(o_ref.dtype)


def _fold_keys(keys, wq):
    n_hp, n_keys, dhalf = keys.shape
    d = wq.shape[0]
    wk = pl.pallas_call(
        _fold_keys_kernel,
        grid=(n_hp,),
        in_specs=[pl.BlockSpec((1, n_keys, dhalf), lambda j: (j, 0, 0)),
                  pl.BlockSpec((d, dhalf), lambda j: (0, j))],
        out_specs=pl.BlockSpec((1, n_keys, d), lambda j: (j, 0, 0)),
        out_shape=jax.ShapeDtypeStruct((n_hp, n_keys, d), BF16),
        compiler_params=_cparams("parallel"),
        name="peer_fold_keys",
    )(keys, wq)
    return wk.reshape(n_hp * n_keys, d)


def _peerq_kernel(x_ref, g_ref, sc_ref, sh_ref, wk_ref, ht_ref, sct_ref, *, m_chunk):
    hf = _norm_mod(x_ref[...], g_ref[...], sc_ref[0], sh_ref[0])
    ht = hf.T.astype(BF16)
    ht_ref[...] = ht
    n_hp, n_keys, _ = sct_ref.shape
    per_chunk = m_chunk // n_keys
    for hp0 in range(0, n_hp, per_chunk):
        res = jnp.dot(wk_ref[hp0 * n_keys:(hp0 + per_chunk) * n_keys, :], ht, preferred_element_type=F32)
        for c in range(per_chunk):
            sct_ref[hp0 + c] = res[c * n_keys:(c + 1) * n_keys, :]


def _peerq(x, g, sc, sh, wk, *, n_keys, seq, tm=512, m_chunk=512):
    t, d = x.shape
    n_hp = wk.shape[0] // n_keys
    tps = seq // tm
    return pl.pallas_call(
        functools.partial(_peerq_kernel, m_chunk=m_chunk),
        grid=(t // tm,),
        in_specs=[
            pl.BlockSpec((tm, d), lambda i: (i, 0)),
            _resident((1, d), lambda i: (0, 0)),
            pl.BlockSpec((1, 1, d), lambda i: (i // tps, 0, 0)),
            pl.BlockSpec((1, 1, d), lambda i: (i // tps, 0, 0)),
            _resident(wk.shape, lambda i: (0, 0)),
        ],
        out_specs=[pl.BlockSpec((d, tm), lambda i: (0, i)),
                   pl.BlockSpec((n_hp, n_keys, tm), lambda i: (0, 0, i))],
        out_shape=[jax.ShapeDtypeStruct((d, t), BF16),
                   jax.ShapeDtypeStruct((n_hp, n_keys, t), F32)],
        compiler_params=_cparams("parallel"),
        name="peer_queries",
    )(x, g, sc, sh, wk)


def _oddeven_mergesort_pairs(n):
    pairs = []
    p = 1
    while p < n:
        k = p
        while k >= 1:
            for j in range(k % p, n - k, 2 * k):
                for i in range(min(k, n - j - k)):
                    if (i + j) // (2 * p) == (i + j + k) // (2 * p):
                        pairs.append((i + j, i + j + k))
            k //= 2
        p *= 2
    return pairs


def _compare_exchange(x, i, j):
    x[i], x[j] = jnp.maximum(x[i], x[j]), jnp.minimum(x[i], x[j])


def _bitonic_sort_desc(x):
    dist = len(x) // 2
    while dist >= 1:
        for k in range(len(x)):
            if k & dist == 0:
                _compare_exchange(x, k, k + dist)
        dist //= 2


def _merge_top(x, y):
    n = len(x)
    z = [jnp.maximum(x[k], y[n - 1 - k]) if n - 1 - k < len(y) else x[k] for k in range(n)]
    _bitonic_sort_desc(z)
    return z


def _top_rows(s, k):
    x = [s[F32_SUBLANES * g:F32_SUBLANES * (g + 1), :] for g in range(k)]
    for i, j in _oddeven_mergesort_pairs(k):
        _compare_exchange(x, i, j)
    shift = F32_SUBLANES // 2
    while shift >= 1:
        x = _merge_top(x, [pltpu.roll(v, shift, 0) for v in x])
        shift //= 2
    return x


def _stats_kernel(sct_ref, e2_ref, thr_ref, a0_ref):
    heads, n_keys, tl = e2_ref.shape
    k = PEER_TOPK
    assert n_keys == k * F32_SUBLANES and heads == F32_SUBLANES
    sub = lax.broadcasted_iota(jnp.int32, (F32_SUBLANES, tl), 0)
    top1, top2, b0 = None, None, []
    for head in range(heads):
        t1 = _top_rows(sct_ref[2 * head], k)
        t2 = _top_rows(sct_ref[2 * head + 1], k)
        b0.append(t2[0])
        if head == 0:
            top1, top2 = t1, t2
        else:
            top1 = [jnp.where(sub == head, new, old) for new, old in zip(t1, top1)]
            top2 = [jnp.where(sub == head, new, old) for new, old in zip(t2, top2)]

    best = [top1[0] + top2[j] for j in range(k)]
    for i in range(1, k // 2):
        best = _merge_top(best, [top1[i] + top2[j] for j in range(k // (i + 1))])
    best = _merge_top(best, [top1[i] + top2[0] for i in range(k // 2, k)])
    z = functools.reduce(lambda acc, v: acc + jnp.exp(v - best[0]), best[1:], jnp.ones_like(best[0]))
    rz = 0.5 / z
    thr_ref[...] = best[k - 1]
    a0_ref[...] = top1[0]
    for head in range(heads):
        e2_ref[head] = jnp.exp(sct_ref[2 * head + 1] - b0[head][0:1, :]) * rz[head:head + 1, :]


def _stats(sct, *, tl=256):
    n_hp, n_keys, t = sct.shape
    heads = n_hp // 2
    return pl.pallas_call(
        _stats_kernel,
        grid=(t // tl,),
        in_specs=[pl.BlockSpec((n_hp, n_keys, tl), lambda i: (0, 0, i))],
        out_specs=[pl.BlockSpec((heads, n_keys, tl), lambda i: (0, 0, i)),
                   pl.BlockSpec((heads, tl), lambda i: (0, i)),
                   pl.BlockSpec((heads, tl), lambda i: (0, i))],
        out_shape=[jax.ShapeDtypeStruct((heads, n_keys, t), F32),
                   jax.ShapeDtypeStruct((heads, t), F32),
                   jax.ShapeDtypeStruct((heads, t), F32)],
        compiler_params=_cparams("parallel"),
        name="peer_gate_stats",
    )(sct)


def _gated_activation(st_ref, a_ref, s1_ref, e1_ref, row0, t0, s2_ref, e2_ref, thr_ref, *, rows):
    heads, n_keys, _ = s2_ref.shape
    n1 = st_ref.shape[0] // n_keys
    lanes = pl.ds(t0, LANES)
    for r0 in range(0, n_keys, rows):
        w = [jnp.zeros((rows, LANES), F32) for _ in range(n1)]
        for head in range(heads):
            s2 = s2_ref[head, r0:r0 + rows, lanes]
            e2 = e2_ref[head, r0:r0 + rows, lanes]
            thr = thr_ref[head:head + 1, lanes]
            for k in range(n1):
                sel = (s1_ref[head, row0 + k:row0 + k + 1, lanes] + s2) >= thr
                w[k] = w[k] + jnp.where(sel, e2, 0.0) * e1_ref[head, row0 + k:row0 + k + 1, lanes]
        for k in range(n1):
            e0 = k * n_keys + r0
            s = st_ref[e0:e0 + rows, lanes]
            act2 = s * (1.0 + lax.erf(s * SQRT_HALF))
            a_ref[e0:e0 + rows, lanes] = (act2 * w[k]).astype(a_ref.dtype)


def _dense_kernel(ht_ref, u_ref, vt_ref, s1p_ref, s1c_ref, s2_ref, e2_ref, thr_ref, a0_ref, x_ref, gate_ref,
                  fin_ref, *refs, rows, n_slices, final_norm, out_starts):
    o_refs = refs[:len(out_starts)]
    acc_ref, st0_ref, st1_ref, a0s_ref, a1s_ref, e1p_ref, e1c_ref = refs[len(out_starts):]
    s = pl.program_id(1)
    heads, _, tm = s2_ref.shape
    half = st0_ref.shape[0]
    d = acc_ref.shape[0]
    n1_half = s1c_ref.shape[1] // 2
    t_slice = tm // n_slices
    u_rows = half // n_slices
    v_rows = d // n_slices

    first = s == 0
    last = s == pl.num_programs(1) - 1

    @pl.when(first)
    def _():
        acc_ref[...] = jnp.zeros_like(acc_ref)

    for head in range(heads):
        e1p_ref[head] = jnp.exp(s1p_ref[head] - a0_ref[head:head + 1, :])
        e1c_ref[head] = jnp.exp(s1c_ref[head] - a0_ref[head:head + 1, :])

    gates = functools.partial(_gated_activation, s2_ref=s2_ref, e2_ref=e2_ref, thr_ref=thr_ref, rows=rows)

    def half_block(u0, v0, st_out, st_in, a_out, a_in, s1_ref, e1_ref, row0, *, score, gate, mix):
        def body(p, carry):
            if score:
                ur = pl.ds(pl.multiple_of(u0 + p * u_rows, u_rows), u_rows)
                st_out[pl.ds(pl.multiple_of(p * u_rows, u_rows), u_rows), :] = jnp.dot(
                    u_ref[ur, :], ht_ref[...], preferred_element_type=F32)
            if mix:
                vr = pl.ds(pl.multiple_of(p * v_rows, v_rows), v_rows)
                acc_ref[vr, :] += jnp.dot(vt_ref[vr, v0:v0 + half], a_in[...], preferred_element_type=F32)
            if gate:
                for c in range(0, t_slice, LANES):
                    gates(st_in, a_out, s1_ref, e1_ref, row0, pl.multiple_of(p * t_slice + c, LANES))
            return carry
        lax.fori_loop(0, n_slices, body, 0)

    first_half = functools.partial(half_block, 0, 0, st0_ref, st1_ref, a1s_ref, a0s_ref, s1p_ref, e1p_ref, n1_half)
    second_half = functools.partial(half_block, half, half, st1_ref, st0_ref, a0s_ref, a1s_ref, s1c_ref, e1c_ref, 0)

    @pl.when(first)
    def _():
        first_half(score=True, gate=False, mix=False)
        second_half(score=True, gate=True, mix=False)

    @pl.when(jnp.logical_not(first | last))
    def _():
        first_half(score=True, gate=True, mix=True)
        second_half(score=True, gate=True, mix=True)

    @pl.when(last)
    def _():
        first_half(score=False, gate=True, mix=True)
        second_half(score=False, gate=False, mix=True)

    def finish(o_ref):
        for d0 in range(0, d, tm):
            o_ref[:, d0:d0 + tm] = x_ref[:, d0:d0 + tm] + gate_ref[0][:, d0:d0 + tm] * acc_ref[d0:d0 + tm, :].T
        if final_norm:
            y = o_ref[...]
            r = lax.rsqrt(jnp.mean(y * y, axis=-1, keepdims=True) + NORM_EPS)
            o_ref[...] = (y * r) * fin_ref[...]

    i = pl.program_id(0)
    last = s == pl.num_programs(1) - 1
    bounds = out_starts[1:] + (pl.num_programs(0),)
    for o_ref, lo, hi in zip(o_refs, out_starts, bounds):
        pl.when(last & (i >= lo) & (i < hi))(functools.partial(finish, o_ref))


def _dense(ht, u, vt, sct, e2, thr, a0, x, gate, fin, *, seq, final_norm, out_rows, tm=512, rows=32,
           n_slices=2):
    t, d = x.shape
    assert sum(out_rows) == t and all(r % tm == 0 for r in out_rows)
    out_specs, out_starts, _ = _row_parts([jax.ShapeDtypeStruct((r, d), F32) for r in out_rows], tm)
    n_exp = u.shape[0]
    n_hp, n_keys, _ = sct.shape
    heads = n_hp // 2
    n1_per_step = 8
    te = n1_per_step * n_keys
    nj = n_exp // te
    tps = seq // tm
    s12 = sct.reshape(heads, 2, n_keys, t)
    once = functools.partial(pl.BlockSpec, pipeline_mode=pl.Buffered(1))
    prev = lambda s: jnp.maximum(s - 1, 0)
    cur = lambda s: jnp.minimum(s, nj - 1)
    return pl.pallas_call(
        functools.partial(_dense_kernel, rows=rows, n_slices=n_slices, final_norm=final_norm,
                          out_starts=out_starts),
        grid=(t // tm, nj + 1),
        in_specs=[
            once((d, tm), lambda i, s: (0, i)),
            pl.BlockSpec((te, d), lambda i, s: (cur(s), 0)),
            pl.BlockSpec((d, te), lambda i, s: (0, prev(s))),
            pl.BlockSpec((heads, None, n1_per_step, tm), lambda i, s: (0, 0, prev(s), i)),
            pl.BlockSpec((heads, None, n1_per_step, tm), lambda i, s: (0, 0, cur(s), i)),
            once((heads, None, n_keys, tm), lambda i, s: (0, 1, 0, i)),
            once((heads, n_keys, tm), lambda i, s: (0, 0, i)),
            once((heads, tm), lambda i, s: (0, i)),
            once((heads, tm), lambda i, s: (0, i)),
            once((tm, d), lambda i, s: (i, 0)),
            pl.BlockSpec((1, 1, d), lambda i, s: (i // tps, 0, 0)),
            _resident((1, d), lambda i, s: (0, 0)),
        ],
        out_specs=out_specs,
        out_shape=[jax.ShapeDtypeStruct((r, d), F32) for r in out_rows],
        scratch_shapes=[pltpu.VMEM((d, tm), F32),
                        pltpu.VMEM((te // 2, tm), F32), pltpu.VMEM((te // 2, tm), F32),
                        pltpu.VMEM((te // 2, tm), BF16), pltpu.VMEM((te // 2, tm), BF16),
                        pltpu.VMEM((heads, n1_per_step, tm), F32), pltpu.VMEM((heads, n1_per_step, tm), F32)],
        compiler_params=_cparams("arbitrary", "arbitrary"),
        name="peer_dense_experts",
    )(ht, u, vt, s12, s12, s12, e2, thr, a0, x, gate, fin)


def _table_cast_kernel(v_ref, o_ref, *, transpose):
    v = v_ref[...]
    o_ref[...] = (v.T if transpose else v).astype(o_ref.dtype)


def _table_cast(tables, layer, *, transpose, tr=512):
    _, e, d = tables.shape
    return pl.pallas_call(
        functools.partial(_table_cast_kernel, transpose=transpose),
        grid=(e // tr,),
        in_specs=[pl.BlockSpec((None, tr, d), lambda j: (layer, j, 0))],
        out_specs=pl.BlockSpec((d, tr), lambda j: (0, j)) if transpose else pl.BlockSpec((tr, d), lambda j: (j, 0)),
        out_shape=jax.ShapeDtypeStruct((d, e) if transpose else (e, d), BF16),
        compiler_params=_cparams("parallel"),
        name="expert_table_cast",
    )(tables)


def kernel(x_prompt, x_sample, c_prompt, c_sample, ada_w, ada_b, norm_mix, norm_ffn, ab_in_w, ab_conv_w, ab_conv_b, ab_out_w, cd_in_w, cd_in_b, cd_conv_w, cd_conv_b, cd_ln_g, cd_ln_b, cd_pool_w, cd_pool_scale, cd_out_w, peer_wq, peer_keys, peer_u, peer_v, final_norm):
    bp, seq, d = x_prompt.shape
    batch = bp + x_sample.shape[0]
    assert x_sample.shape[1] == seq
    t = batch * seq
    depth = ada_w.shape[0]
    d_a = ab_conv_w.shape[2]
    d_b = ab_in_w.shape[2] - 3 * d_a
    d_c = cd_conv_w.shape[2]
    d_d = cd_in_w.shape[2] - 2 * d_c
    assert d_a == d_b == d_c == d_d, "column blocks of the combined projections are addressed by block index"
    heads, _, n_keys, dhalf = peer_keys.shape[1:]

    group_rows = [bp * seq, (batch - bp) * seq]
    xs = [x_prompt.reshape(group_rows[0], d), x_sample.reshape(group_rows[1], d)]
    c = jnp.concatenate([c_prompt, c_sample], axis=0)
    c_pad = jnp.pad(c, ((0, -batch % 8), (0, 0)))
    mod = _modulation(c_pad, ada_w, ada_b)[:, :batch].reshape(depth, batch, 6, 1, d)

    dcat, cs = _dft_tables(seq, HEAD_DIM)
    row = lambda v: v.reshape(1, -1)

    for i in range(depth):
        sh_m, sc_m, g_m, sh_f, sc_f, g_f = [mod[i, :, k] for k in range(6)]
        j = i // 2
        if i % 2 == 0:
            w_in = ab_in_w[j].astype(BF16)
            p = _inproj(xs, row(norm_mix[i]), sc_m, sh_m, w_in, jnp.zeros((1, w_in.shape[1]), F32), seq=seq)
            left = _gconv3(p, ab_conv_w[j], row(ab_conv_b[j]), seq=seq, d_a=d_a)
            ab = _dft_channel(p, cs, batch=batch, seq=seq, d_b=d_b, col_block=3)
            right = _dft_seq(dcat, ab.reshape(batch, 2 * seq, d_b)).reshape(t, d_b)
            w_out = ab_out_w[j].astype(BF16)
        else:
            p = _inproj(xs, row(norm_mix[i]), sc_m, sh_m, cd_in_w[j].astype(BF16), row(cd_in_b[j]), seq=seq)
            left = _conf_conv(p, cd_conv_w[j], row(cd_conv_b[j]), row(cd_ln_g[j]), row(cd_ln_b[j]),
                              seq=seq, d_c=d_c)
            right = _pool(p, cd_pool_w[j].astype(BF16), row(cd_pool_scale[j]), seq=seq, d_d=d_d, col_block=2)
            w_out = cd_out_w[j].astype(BF16)
        x = _outproj(left, right, w_out, xs, g_m, seq=seq)

        wk = _fold_keys(peer_keys[i].reshape(heads * 2, n_keys, dhalf), peer_wq[i])
        ht, sct = _peerq(x, row(norm_ffn[i]), sc_f, sh_f, wk, n_keys=n_keys, seq=seq)
        e2, thr, a0 = _stats(sct)
        last = i == depth - 1
        xs = _dense(ht, _table_cast(peer_u, i, transpose=False), _table_cast(peer_v, i, transpose=True),
                    sct, e2, thr, a0, x, g_f,
                    row(final_norm), seq=seq, final_norm=last, out_rows=group_rows if last else [t])

    return tuple(a.reshape(-1, seq, d) for a in xs)
```
